```python
import jax, jax.numpy as jnp
from jax import lax
import numpy as np

D_MODEL = 1024
BATCH = 8
SEQ = 4096
DEPTH = 1

POOL_WIDTH = D_MODEL
POOL_WINDOWS = (2, 4, 8, 16)
N_POOL_GROUPS = len(POOL_WINDOWS)
POOL_GROUP_DIM = POOL_WIDTH // N_POOL_GROUPS
CONV_WIDTH = D_MODEL
CONV_KERNEL = 31
N_BRANCHES = 2
D_IN = POOL_WIDTH + 2 * CONV_WIDTH + N_BRANCHES * D_MODEL
D_FF = ((8 * D_MODEL // 3 + 255) // 256) * 256
LN_EPS = 1e-5
DEEPNORM_ALPHA = (2.0 * DEPTH) ** 0.25
DEEPNORM_BETA = (8.0 * DEPTH) ** -0.25

kernel_name = "hybrid_pool_conformer_gated_deepnorm"


def layer_norm(x, g, b):
    xf = x.astype(jnp.float32)
    mu = jnp.mean(xf, axis=-1, keepdims=True)
    var = jnp.mean(jnp.square(xf - mu), axis=-1, keepdims=True)
    y = (xf - mu) * lax.rsqrt(var + LN_EPS)
    return (y * g.astype(jnp.float32) + b.astype(jnp.float32)).astype(x.dtype)


def multiscale_pool(xp):
    B, S, _ = xp.shape
    xg = xp.astype(jnp.float32).reshape(B, S, N_POOL_GROUPS, POOL_GROUP_DIM)
    cs = jnp.concatenate([jnp.zeros((B, 1, N_POOL_GROUPS, POOL_GROUP_DIM), jnp.float32),
                          jnp.cumsum(xg, axis=1)], axis=1)
    t = jnp.arange(S, dtype=jnp.int32)
    outs = []
    for g, w in enumerate(POOL_WINDOWS):
        lo = jnp.clip(t - w // 2, 0, S)
        hi = jnp.clip(t + w // 2, 0, S)
        cnt = (hi - lo).astype(jnp.float32)
        window_sum = cs[:, hi, g] - cs[:, lo, g]
        outs.append(window_sum / cnt[None, :, None] - xg[:, :, g])
    return jnp.stack(outs, axis=2).astype(xp.dtype)


def depthwise_conv(h, w, b):
    pad = (CONV_KERNEL - 1) // 2
    y = lax.conv_general_dilated(
        h, w[:, None, :].astype(h.dtype), window_strides=(1,), padding=[(pad, pad)],
        dimension_numbers=("NWC", "WIO", "NWC"), feature_group_count=h.shape[-1])
    return y + b


def setup_inputs(seed: int = 0) -> dict:
    key = jax.random.key(seed)
    ks = jax.random.split(key, 20)
    L = DEPTH
    nrm = lambda k, shape, s: jax.random.normal(k, shape, jnp.float32) * s
    return {
        "x": jax.random.normal(ks[0], (BATCH, SEQ, D_MODEL), jnp.float32),
        "w_in": nrm(ks[1], (L, D_MODEL, D_IN), D_MODEL ** -0.5),
        "b_in": nrm(ks[2], (L, D_IN), 0.02),
        "pool_w": nrm(ks[3], (L, N_POOL_GROUPS, POOL_GROUP_DIM, POOL_GROUP_DIM), POOL_GROUP_DIM ** -0.5),
        "pool_scale": 1.0 + nrm(ks[4], (L, POOL_WIDTH), 0.02),
        "w_pool_proj": nrm(ks[5], (L, POOL_WIDTH, D_MODEL), POOL_WIDTH ** -0.5),
        "conv_w": nrm(ks[6], (L, CONV_KERNEL, CONV_WIDTH), CONV_KERNEL ** -0.5),
        "conv_b": nrm(ks[7], (L, CONV_WIDTH), 0.02),
        "conv_ln_g": 1.0 + nrm(ks[8], (L, CONV_WIDTH), 0.02),
        "conv_ln_b": nrm(ks[9], (L, CONV_WIDTH), 0.02),
        "w_conv_proj": nrm(ks[10], (L, CONV_WIDTH, D_MODEL), CONV_WIDTH ** -0.5),
        "w_out": nrm(ks[11], (L, D_MODEL, D_MODEL), D_MODEL ** -0.5 * DEEPNORM_BETA),
        "ln1_g": 1.0 + nrm(ks[12], (L, D_MODEL), 0.02),
        "ln1_b": nrm(ks[13], (L, D_MODEL), 0.02),
        "w_ffn_in": nrm(ks[14], (L, D_MODEL, 2 * D_FF), D_MODEL ** -0.5),
        "w_ffn_down": nrm(ks[15], (L, D_FF, D_MODEL), D_FF ** -0.5 * DEEPNORM_BETA),
        "ln2_g": 1.0 + nrm(ks[16], (L, D_MODEL), 0.02),
        "ln2_b": nrm(ks[17], (L, D_MODEL), 0.02),
    }


def reference(x, w_in, b_in, pool_w, pool_scale, w_pool_proj, conv_w, conv_b,
              conv_ln_g, conv_ln_b, w_conv_proj, w_out, ln1_g, ln1_b,
              w_ffn_in, w_ffn_down, ln2_g, ln2_b):
    B, S, _ = x.shape
    for l in range(DEPTH):
        u = jnp.einsum("bsd,de->bse", x, w_in[l]) + b_in[l]
        u_pool = u[..., :POOL_WIDTH]
        u_conv = u[..., POOL_WIDTH:POOL_WIDTH + 2 * CONV_WIDTH]
        u_gate = u[..., POOL_WIDTH + 2 * CONV_WIDTH:]
        gate_pool = jax.nn.sigmoid(u_gate[..., :D_MODEL])
        gate_conv = jax.nn.sigmoid(u_gate[..., D_MODEL:])

        pooled = multiscale_pool(u_pool)
        pooled = jnp.einsum("bsgc,gce->bsge", pooled, pool_w[l]).reshape(B, S, POOL_WIDTH)
        y_pool = jnp.einsum("bsp,pd->bsd", pooled * pool_scale[l], w_pool_proj[l])

        h = u_conv[..., :CONV_WIDTH] * jax.nn.sigmoid(u_conv[..., CONV_WIDTH:])
        h = depthwise_conv(h, conv_w[l], conv_b[l])
        h = jax.nn.silu(layer_norm(h, conv_ln_g[l], conv_ln_b[l]))
        y_conv = jnp.einsum("bsc,cd->bsd", h, w_conv_proj[l])

        merged = gate_pool * y_pool + gate_conv * y_conv
        mix_out = jnp.einsum("bsd,de->bse", merged, w_out[l])
        x = layer_norm(DEEPNORM_ALPHA * x + mix_out, ln1_g[l], ln1_b[l])

        gu = jnp.einsum("bsd,df->bsf", x, w_ffn_in[l])
        hid = jax.nn.silu(gu[..., :D_FF]) * gu[..., D_FF:]
        ffn_out = jnp.einsum("bsf,fd->bsd", hid, w_ffn_down[l])
        x = layer_norm(DEEPNORM_ALPHA * x + ffn_out, ln2_g[l], ln2_b[l])
    return x
```

```python
import functools

import jax
import jax.numpy as jnp
from jax import lax
from jax.experimental import pallas as pl
from jax.experimental.pallas import tpu as pltpu

POOL_WINDOWS = (2, 4, 8, 16)
CONV_KERNEL = 31
LN_EPS = 1e-5

LANES = 128
MXU_COLS = 256
HALO = 16
SEQ_TILE = 512
ROW_TILE = 512
LN_ROWS = 64
VMEM_LIMIT_BYTES = 56 * 1024 * 1024


def _layer_norm(v, g, b):
    mu = jnp.mean(v, axis=-1, keepdims=True)
    d = v - mu
    var = jnp.mean(d * d, axis=-1, keepdims=True)
    return d * lax.rsqrt(var + LN_EPS) * g + b


def _dot(a, b):
    return jnp.dot(a, b, preferred_element_type=jnp.float32)


def _mix_kernel(x_ref, xp_ref, xn_ref, w_in_ref, b_in_ref, pool_w_ref, pool_scale_ref,
                w_pp_ref, conv_w_ref, conv_b_ref, cln_g_ref, cln_b_ref, w_cp_ref,
                w_out_ref, ln_g_ref, ln_b_ref, o_ref,
                xh_scr, up_scr, h_scr, pooled_scr, pw_scr, hc_scr, hs_scr, mg_scr,
                *, seq_len, alpha):
    ts = x_ref.shape[1]
    d = x_ref.shape[2]
    rows = ts + 2 * HALO
    n_slab = d // LANES
    n_chunk = d // MXU_COLS
    group_dim = d // len(POOL_WINDOWS)
    row0 = pl.program_id(1) * ts - HALO

    xh_scr[0:HALO, :] = xp_ref[0].astype(jnp.bfloat16)
    xh_scr[HALO:HALO + ts, :] = x_ref[0].astype(jnp.bfloat16)
    xh_scr[HALO + ts:rows, :] = xn_ref[0].astype(jnp.bfloat16)

    def valid_rows(width):
        r = lax.broadcasted_iota(jnp.int32, (rows, width), 0) + row0
        return (r >= 0) & (r < seq_len)

    valid_c = valid_rows(MXU_COLS)
    for c in range(n_chunk):
        lo = c * MXU_COLS
        up = _dot(xh_scr[...], w_in_ref[:, lo:lo + MXU_COLS]) + b_in_ref[:, lo:lo + MXU_COLS]
        up = jnp.where(valid_c, up, 0.0)
        for s in range(MXU_COLS // LANES):
            up_scr[c * (MXU_COLS // LANES) + s] = up[:, s * LANES:(s + 1) * LANES]

    t = lax.broadcasted_iota(jnp.int32, (ts, LANES), 0) + (row0 + HALO)
    for g, win in enumerate(POOL_WINDOWS):
        half = win // 2
        cnt = jnp.minimum(t + half, seq_len) - jnp.maximum(t - half, 0)
        inv_cnt = 1.0 / cnt.astype(jnp.float32)
        for s in range(group_dim // LANES):
            j = g * (group_dim // LANES) + s
            wsum = up_scr[j, pl.ds(HALO - half, ts), :]
            for k in range(1, win):
                wsum = wsum + up_scr[j, pl.ds(HALO - half + k, ts), :]
            pooled = wsum * inv_cnt - up_scr[j, pl.ds(HALO, ts), :]
            pooled_scr[:, j * LANES:(j + 1) * LANES] = pooled.astype(jnp.bfloat16)
    for g in range(len(POOL_WINDOWS)):
        lo = g * group_dim
        pg = _dot(pooled_scr[:, lo:lo + group_dim], pool_w_ref[g])
        pw_scr[:, lo:lo + group_dim] = (pg * pool_scale_ref[:, lo:lo + group_dim]).astype(jnp.bfloat16)
    y_pool = _dot(pw_scr[...], w_pp_ref[...])

    conv_off = d + 0
    gate_off = 2 * d
    for c in range(n_chunk):
        lo = c * MXU_COLS
        a = _dot(xh_scr[...], w_in_ref[:, conv_off + lo:conv_off + lo + MXU_COLS])
        a = a + b_in_ref[:, conv_off + lo:conv_off + lo + MXU_COLS]
        gt = _dot(xh_scr[...], w_in_ref[:, gate_off + lo:gate_off + lo + MXU_COLS])
        gt = gt + b_in_ref[:, gate_off + lo:gate_off + lo + MXU_COLS]
        h = jnp.where(valid_c, a * jax.nn.sigmoid(gt), 0.0)
        for s in range(MXU_COLS // LANES):
            h_scr[c * (MXU_COLS // LANES) + s] = h[:, s * LANES:(s + 1) * LANES]
    pad = (CONV_KERNEL - 1) // 2
    for j in range(n_slab):
        cw = conv_w_ref[:, j * LANES:(j + 1) * LANES]
        acc = h_scr[j, pl.ds(HALO - pad, ts), :] * cw[0:1, :]
        for k in range(1, CONV_KERNEL):
            acc = acc + h_scr[j, pl.ds(HALO - pad + k, ts), :] * cw[k:k + 1, :]
        hc_scr[:, j * LANES:(j + 1) * LANES] = acc + conv_b_ref[:, j * LANES:(j + 1) * LANES]
    for r in range(ts // LN_ROWS):
        blk = hc_scr[r * LN_ROWS:(r + 1) * LN_ROWS, :]
        y = _layer_norm(blk, cln_g_ref[...], cln_b_ref[...])
        hs_scr[r * LN_ROWS:(r + 1) * LN_ROWS, :] = (y * jax.nn.sigmoid(y)).astype(jnp.bfloat16)
    y_conv = _dot(hs_scr[...], w_cp_ref[...])

    gp_off = 3 * d
    gc_off = 4 * d
    xm = xh_scr[HALO:HALO + ts, :]
    for c in range(n_chunk):
        lo = c * MXU_COLS
        ugp = _dot(xm, w_in_ref[:, gp_off + lo:gp_off + lo + MXU_COLS]) + b_in_ref[:, gp_off + lo:gp_off + lo + MXU_COLS]
        ugc = _dot(xm, w_in_ref[:, gc_off + lo:gc_off + lo + MXU_COLS]) + b_in_ref[:, gc_off + lo:gc_off + lo + MXU_COLS]
        merged = (jax.nn.sigmoid(ugp) * y_pool[:, lo:lo + MXU_COLS]
                  + jax.nn.sigmoid(ugc) * y_conv[:, lo:lo + MXU_COLS])
        mg_scr[:, lo:lo + MXU_COLS] = merged.astype(jnp.bfloat16)
    mix = _dot(mg_scr[...], w_out_ref[...])
    for r in range(ts // LN_ROWS):
        sl = slice(r * LN_ROWS, (r + 1) * LN_ROWS)
        res = alpha * x_ref[0, sl, :] + mix[sl, :]
        o_ref[0, sl, :] = _layer_norm(res, ln_g_ref[...], ln_b_ref[...])


def _ffn_kernel(x_ref, w_in_ref, w_down_ref, ln_g_ref, ln_b_ref, o_ref, xb_scr, hid_scr, *, alpha):
    tm = x_ref.shape[0]
    d_ff = w_down_ref.shape[0]
    xb_scr[...] = x_ref[...].astype(jnp.bfloat16)
    for c in range(d_ff // MXU_COLS):
        lo = c * MXU_COLS
        g = _dot(xb_scr[...], w_in_ref[:, lo:lo + MXU_COLS])
        u = _dot(xb_scr[...], w_in_ref[:, d_ff + lo:d_ff + lo + MXU_COLS])
        hid_scr[:, lo:lo + MXU_COLS] = (g * jax.nn.sigmoid(g) * u).astype(jnp.bfloat16)
    ffn = _dot(hid_scr[...], w_down_ref[...])
    for r in range(tm // LN_ROWS):
        sl = slice(r * LN_ROWS, (r + 1) * LN_ROWS)
        res = alpha * x_ref[sl, :] + ffn[sl, :]
        o_ref[sl, :] = _layer_norm(res, ln_g_ref[...], ln_b_ref[...])


def _resident(shape):
    zeros = (0,) * len(shape)
    return pl.BlockSpec(shape, lambda *_: zeros, pipeline_mode=pl.Buffered(1))


def _token_mix(x, w_in, b_in, pool_w, pool_scale, w_pp, conv_w, conv_b, cln_g, cln_b, w_cp,
               w_out, ln_g, ln_b, alpha):
    b, s, d = x.shape
    ts = SEQ_TILE
    assert s % ts == 0 and ts % HALO == 0 and d % MXU_COLS == 0
    assert d // len(POOL_WINDOWS) % LANES == 0
    rows = ts + 2 * HALO
    halo_blocks = ts // HALO
    last_halo_block = s // HALO - 1
    residents = (w_in, b_in, pool_w, pool_scale, w_pp, conv_w, conv_b, cln_g, cln_b, w_cp, w_out, ln_g, ln_b)
    return pl.pallas_call(
        functools.partial(_mix_kernel, seq_len=s, alpha=alpha),
        out_shape=jax.ShapeDtypeStruct((b, s, d), jnp.float32),
        grid=(b, s // ts),
        in_specs=[
            pl.BlockSpec((1, ts, d), lambda bi, i: (bi, i, 0)),
            pl.BlockSpec((1, HALO, d), lambda bi, i: (bi, jnp.maximum(i * halo_blocks - 1, 0), 0)),
            pl.BlockSpec((1, HALO, d), lambda bi, i: (bi, jnp.minimum((i + 1) * halo_blocks, last_halo_block), 0)),
        ] + [_resident(a.shape) for a in residents],
        out_specs=pl.BlockSpec((1, ts, d), lambda bi, i: (bi, i, 0)),
        scratch_shapes=[
            pltpu.VMEM((rows, d), jnp.bfloat16),
            pltpu.VMEM((d // LANES, rows, LANES), jnp.float32),
            pltpu.VMEM((d // LANES, rows, LANES), jnp.float32),
            pltpu.VMEM((ts, d), jnp.bfloat16),
            pltpu.VMEM((ts, d), jnp.bfloat16),
            pltpu.VMEM((ts, d), jnp.float32),
            pltpu.VMEM((ts, d), jnp.bfloat16),
            pltpu.VMEM((ts, d), jnp.bfloat16),
        ],
        compiler_params=pltpu.CompilerParams(
            dimension_semantics=("arbitrary", "arbitrary"),
            vmem_limit_bytes=VMEM_LIMIT_BYTES),
        name="token_mix",
    )(x, x, x, *residents)


def _ffn(x2d, w_in, w_down, ln_g, ln_b, alpha):
    t, d = x2d.shape
    tm = ROW_TILE
    d_ff = w_down.shape[0]
    assert t % tm == 0 and d_ff % MXU_COLS == 0
    residents = (w_in, w_down, ln_g, ln_b)
    return pl.pallas_call(
        functools.partial(_ffn_kernel, alpha=alpha),
        out_shape=jax.ShapeDtypeStruct((t, d), jnp.float32),
        grid=(t // tm,),
        in_specs=[pl.BlockSpec((tm, d), lambda i: (i, 0))] + [_resident(a.shape) for a in residents],
        out_specs=pl.BlockSpec((tm, d), lambda i: (i, 0)),
        scratch_shapes=[
            pltpu.VMEM((tm, d), jnp.bfloat16),
            pltpu.VMEM((tm, d_ff), jnp.bfloat16),
        ],
        compiler_params=pltpu.CompilerParams(
            dimension_semantics=("arbitrary",),
            vmem_limit_bytes=VMEM_LIMIT_BYTES),
        name="swiglu_ffn",
    )(x2d, *residents)


def kernel(x, w_in, b_in, pool_w, pool_scale, w_pool_proj, conv_w, conv_b, conv_ln_g, conv_ln_b, w_conv_proj, w_out, ln1_g, ln1_b, w_ffn_in, w_ffn_down, ln2_g, ln2_b):
    b, s, d = x.shape
    depth = w_in.shape[0]
    alpha = (2.0 * depth) ** 0.25
    bf = lambda w: w.astype(jnp.bfloat16)
    row = lambda v: v.reshape(1, -1)
    for l in range(depth):
        x = _token_mix(
            x, bf(w_in[l]), row(b_in[l]), bf(pool_w[l]), row(pool_scale[l]), bf(w_pool_proj[l]),
            conv_w[l], row(conv_b[l]), row(conv_ln_g[l]), row(conv_ln_b[l]), bf(w_conv_proj[l]),
            bf(w_out[l]), row(ln1_g[l]), row(ln1_b[l]), alpha)
        x = _ffn(x.reshape(b * s, d), bf(w_ffn_in[l]), bf(w_ffn_down[l]), row(ln2_g[l]), row(ln2_b[l]),
                 alpha).reshape(b, s, d)
    return x
```

```python
import functools

import jax
import jax.numpy as jnp
from jax import lax
from jax.experimental import pallas as pl
from jax.experimental.pallas import tpu as pltpu

POOL_WINDOWS = (2, 4, 8, 16)
CONV_KERNEL = 31
LN_EPS = 1e-5

LANES = 128
MXU_COLS = 256
HALO = 16
SEQ_TILE = 512
ROW_TILE = 1024
LN_ROWS = 16
VMEM_LIMIT_BYTES = 56 * 1024 * 1024


def _layer_norm(v, g, b):
    mu = jnp.mean(v, axis=-1, keepdims=True)
    d = v - mu
    var = jnp.mean(d * d, axis=-1, keepdims=True)
    return d * lax.rsqrt(var + LN_EPS) * g + b


def _dot(a, b):
    return jnp.dot(a, b, preferred_element_type=jnp.float32)


def _mix_kernel(x_ref, xp_ref, xn_ref, w_in_ref, b_in_ref, pool_w_ref, pool_scale_ref,
                w_pp_ref, conv_w_ref, conv_b_ref, cln_g_ref, cln_b_ref, w_cp_ref,
                w_out_ref, ln_g_ref, ln_b_ref, o_ref,
                xh_scr, up_scr, h_scr, pooled_scr, pw_scr, hc_scr, hs_scr, mg_scr,
                *, seq_len, alpha):
    ts = x_ref.shape[1]
    d = x_ref.shape[2]
    rows = ts + 2 * HALO
    n_slab = d // LANES
    n_chunk = d // MXU_COLS
    group_dim = d // len(POOL_WINDOWS)
    row0 = pl.program_id(1) * ts - HALO

    xh_scr[0:HALO, :] = xp_ref[0].astype(jnp.bfloat16)
    xh_scr[HALO:HALO + ts, :] = x_ref[0].astype(jnp.bfloat16)
    xh_scr[HALO + ts:rows, :] = xn_ref[0].astype(jnp.bfloat16)

    def valid_rows(width):
        r = lax.broadcasted_iota(jnp.int32, (rows, width), 0) + row0
        return (r >= 0) & (r < seq_len)

    valid_c = valid_rows(MXU_COLS)
    for c in range(n_chunk):
        lo = c * MXU_COLS
        up = _dot(xh_scr[...], w_in_ref[:, lo:lo + MXU_COLS]) + b_in_ref[:, lo:lo + MXU_COLS]
        up = jnp.where(valid_c, up, 0.0)
        for s in range(MXU_COLS // LANES):
            up_scr[c * (MXU_COLS // LANES) + s] = up[:, s * LANES:(s + 1) * LANES]

    t = lax.broadcasted_iota(jnp.int32, (ts, LANES), 0) + (row0 + HALO)
    for g, win in enumerate(POOL_WINDOWS):
        half = win // 2
        cnt = jnp.minimum(t + half, seq_len) - jnp.maximum(t - half, 0)
        inv_cnt = 1.0 / cnt.astype(jnp.float32)
        for s in range(group_dim // LANES):
            j = g * (group_dim // LANES) + s
            wsum = up_scr[j, pl.ds(HALO - half, ts), :]
            for k in range(1, win):
                wsum = wsum + up_scr[j, pl.ds(HALO - half + k, ts), :]
            pooled = wsum * inv_cnt - up_scr[j, pl.ds(HALO, ts), :]
            pooled_scr[:, j * LANES:(j + 1) * LANES] = pooled.astype(jnp.bfloat16)
    for g in range(len(POOL_WINDOWS)):
        lo = g * group_dim
        pg = _dot(pooled_scr[:, lo:lo + group_dim], pool_w_ref[g])
        pw_scr[:, lo:lo + group_dim] = (pg * pool_scale_ref[:, lo:lo + group_dim]).astype(jnp.bfloat16)
    y_pool = _dot(pw_scr[...], w_pp_ref[...])

    conv_off = d + 0
    gate_off = 2 * d
    for c in range(n_chunk):
        lo = c * MXU_COLS
        a = _dot(xh_scr[...], w_in_ref[:, conv_off + lo:conv_off + lo + MXU_COLS])
        a = a + b_in_ref[:, conv_off + lo:conv_off + lo + MXU_COLS]
        gt = _dot(xh_scr[...], w_in_ref[:, gate_off + lo:gate_off + lo + MXU_COLS])
        gt = gt + b_in_ref[:, gate_off + lo:gate_off + lo + MXU_COLS]
        h = jnp.where(valid_c, a * jax.nn.sigmoid(gt), 0.0)
        for s in range(MXU_COLS // LANES):
            h_scr[c * (MXU_COLS // LANES) + s] = h[:, s * LANES:(s + 1) * LANES]
    pad = (CONV_KERNEL - 1) // 2
    for j in range(n_slab):
        cw = conv_w_ref[:, j * LANES:(j + 1) * LANES]
        acc = h_scr[j, pl.ds(HALO - pad, ts), :] * cw[0:1, :]
        for k in range(1, CONV_KERNEL):
            acc = acc + h_scr[j, pl.ds(HALO - pad + k, ts), :] * cw[k:k + 1, :]
        hc_scr[:, j * LANES:(j + 1) * LANES] = acc + conv_b_ref[:, j * LANES:(j + 1) * LANES]
    for r in range(ts // LN_ROWS):
        blk = hc_scr[r * LN_ROWS:(r + 1) * LN_ROWS, :]
        y = _layer_norm(blk, cln_g_ref[...], cln_b_ref[...])
        half_y = (0.5 * y).astype(jnp.bfloat16)
        hs_scr[r * LN_ROWS:(r + 1) * LN_ROWS, :] = half_y + half_y * jnp.tanh(half_y)
    y_conv = _dot(hs_scr[...], w_cp_ref[...])

    gp_off = 3 * d
    gc_off = 4 * d
    xm = xh_scr[HALO:HALO + ts, :]
    for c in range(n_chunk):
        lo = c * MXU_COLS
        ugp = _dot(xm, w_in_ref[:, gp_off + lo:gp_off + lo + MXU_COLS]) + b_in_ref[:, gp_off + lo:gp_off + lo + MXU_COLS]
        ugc = _dot(xm, w_in_ref[:, gc_off + lo:gc_off + lo + MXU_COLS]) + b_in_ref[:, gc_off + lo:gc_off + lo + MXU_COLS]
        merged = (jax.nn.sigmoid(ugp) * y_pool[:, lo:lo + MXU_COLS]
                  + jax.nn.sigmoid(ugc) * y_conv[:, lo:lo + MXU_COLS])
        mg_scr[:, lo:lo + MXU_COLS] = merged.astype(jnp.bfloat16)
    mix = _dot(mg_scr[...], w_out_ref[...])
    for r in range(ts // LN_ROWS):
        sl = slice(r * LN_ROWS, (r + 1) * LN_ROWS)
        res = alpha * x_ref[0, sl, :] + mix[sl, :]
        o_ref[0, sl, :] = _layer_norm(res, ln_g_ref[...], ln_b_ref[...])


def _ffn_kernel(x_ref, w_in_ref, w_down_ref, ln_g_ref, ln_b_ref, o_ref, xb_scr, hid_scr, *, alpha):
    tm = x_ref.shape[0]
    d_ff = w_down_ref.shape[0]
    xb_scr[...] = x_ref[...].astype(jnp.bfloat16)
    for c in range(d_ff // MXU_COLS):
        lo = c * MXU_COLS
        g = _dot(xb_scr[...], w_in_ref[:, lo:lo + MXU_COLS])
        u = _dot(xb_scr[...], w_in_ref[:, d_ff + lo:d_ff + lo + MXU_COLS])
        hid_scr[:, lo:lo + MXU_COLS] = (g * jax.nn.sigmoid(g) * u).astype(jnp.bfloat16)
    ffn = _dot(hid_scr[...], w_down_ref[...])
    for r in range(tm // LN_ROWS):
        sl = slice(r * LN_ROWS, (r + 1) * LN_ROWS)
        res = alpha * x_ref[sl, :] + ffn[sl, :]
        o_ref[sl, :] = _layer_norm(res, ln_g_ref[...], ln_b_ref[...])


def _resident(shape):
    zeros = (0,) * len(shape)
    return pl.BlockSpec(shape, lambda *_: zeros, pipeline_mode=pl.Buffered(1))


def _token_mix(x, w_in, b_in, pool_w, pool_scale, w_pp, conv_w, conv_b, cln_g, cln_b, w_cp,
               w_out, ln_g, ln_b, alpha):
    b, s, d = x.shape
    ts = SEQ_TILE
    assert s % ts == 0 and ts % HALO == 0 and d % MXU_COLS == 0
    assert d // len(POOL_WINDOWS) % LANES == 0
    rows = ts + 2 * HALO
    halo_blocks = ts // HALO
    last_halo_block = s // HALO - 1
    residents = (w_in, b_in, pool_w, pool_scale, w_pp, conv_w, conv_b, cln_g, cln_b, w_cp, w_out, ln_g, ln_b)
    return pl.pallas_call(
        functools.partial(_mix_kernel, seq_len=s, alpha=alpha),
        out_shape=jax.ShapeDtypeStruct((b, s, d), jnp.float32),
        grid=(b, s // ts),
        in_specs=[
            pl.BlockSpec((1, ts, d), lambda bi, i: (bi, i, 0)),
            pl.BlockSpec((1, HALO, d), lambda bi, i: (bi, jnp.maximum(i * halo_blocks - 1, 0), 0)),
            pl.BlockSpec((1, HALO, d), lambda bi, i: (bi, jnp.minimum((i + 1) * halo_blocks, last_halo_block), 0)),
        ] + [_resident(a.shape) for a in residents],
        out_specs=pl.BlockSpec((1, ts, d), lambda bi, i: (bi, i, 0)),
        scratch_shapes=[
            pltpu.VMEM((rows, d), jnp.bfloat16),
            pltpu.VMEM((d // LANES, rows, LANES), jnp.float32),
            pltpu.VMEM((d // LANES, rows, LANES), jnp.float32),
            pltpu.VMEM((ts, d), jnp.bfloat16),
            pltpu.VMEM((ts, d), jnp.bfloat16),
            pltpu.VMEM((ts, d), jnp.float32),
            pltpu.VMEM((ts, d), jnp.bfloat16),
            pltpu.VMEM((ts, d), jnp.bfloat16),
        ],
        compiler_params=pltpu.CompilerParams(
            dimension_semantics=("arbitrary", "arbitrary"),
            vmem_limit_bytes=VMEM_LIMIT_BYTES),
        name="token_mix",
    )(x, x, x, *residents)


def _ffn(x2d, w_in, w_down, ln_g, ln_b, alpha):
    t, d = x2d.shape
    tm = ROW_TILE
    d_ff = w_down.shape[0]
    assert t % tm == 0 and d_ff % MXU_COLS == 0
    residents = (w_in, w_down, ln_g, ln_b)
    return pl.pallas_call(
        functools.partial(_ffn_kernel, alpha=alpha),
        out_shape=jax.ShapeDtypeStruct((t, d), jnp.float32),
        grid=(t // tm,),
        in_specs=[pl.BlockSpec((tm, d), lambda i: (i, 0))] + [_resident(a.shape) for a in residents],
        out_specs=pl.BlockSpec((tm, d), lambda i: (i, 0)),
        scratch_shapes=[
            pltpu.VMEM((tm, d), jnp.bfloat16),
            pltpu.VMEM((tm, d_ff), jnp.bfloat16),
        ],
        compiler_params=pltpu.CompilerParams(
            dimension_semantics=("arbitrary",),
            vmem_limit_bytes=VMEM_LIMIT_BYTES),
        name="swiglu_ffn",
    )(x2d, *residents)


def kernel(x, w_in, b_in, pool_w, pool_scale, w_pool_proj, conv_w, conv_b, conv_ln_g, conv_ln_b, w_conv_proj, w_out, ln1_g, ln1_b, w_ffn_in, w_ffn_down, ln2_g, ln2_b):
    b, s, d = x.shape
    depth = w_in.shape[0]
    alpha = (2.0 * depth) ** 0.25
    bf = lambda w: w.astype(jnp.bfloat16)
    row = lambda v: v.reshape(1, -1)
    for l in range(depth):
        x = _token_mix(
            x, bf(w_in[l]), row(b_in[l]), bf(pool_w[l]), row(pool_scale[l]), bf(w_pool_proj[l]),
            conv_w[l], row(conv_b[l]), row(conv_ln_g[l]), row(conv_ln_b[l]), bf(w_conv_proj[l]),
            bf(w_out[l]), row(ln1_g[l]), row(ln1_b[l]), alpha)
        x = _ffn(x.reshape(b * s, d), bf(w_ffn_in[l]), bf(w_ffn_down[l]), row(ln2_g[l]), row(ln2_b[l]),
                 alpha).reshape(b, s, d)
    return x
```

```python
import functools

import jax
import jax.numpy as jnp
from jax import lax
from jax.experimental import pallas as pl
from jax.experimental.pallas import tpu as pltpu

POOL_WINDOWS = (2, 4, 8, 16)
CONV_KERNEL = 31
LN_EPS = 1e-5

LANES = 128
MXU_COLS = 256
HALO = 16
SEQ_TILE = 512
ROW_TILE = 1024
LN_ROWS = 16
VMEM_LIMIT_BYTES = 56 * 1024 * 1024


def _layer_norm(v, g, b):
    mu = jnp.mean(v, axis=-1, keepdims=True)
    d = v - mu
    var = jnp.mean(d * d, axis=-1, keepdims=True)
    return d * lax.rsqrt(var + LN_EPS) * g + b


def _dot(a, b):
    return jnp.dot(a, b, preferred_element_type=jnp.float32)


def _mix_kernel(x_ref, xp_ref, xn_ref, w_in_ref, b_in_ref, pool_w_ref, pool_scale_ref,
                w_pp_ref, conv_w_ref, conv_b_ref, cln_g_ref, cln_b_ref, w_cp_ref,
                w_out_ref, ln_g_ref, ln_b_ref, o_ref,
                xh_scr, up_scr, h_scr, pooled_scr, pw_scr, hc_scr, hs_scr, mg_scr, gp_scr, gc_scr,
                *, seq_len, alpha):
    ts = x_ref.shape[1]
    d = x_ref.shape[2]
    rows = ts + 2 * HALO
    n_slab = d // LANES
    n_chunk = d // MXU_COLS
    group_dim = d // len(POOL_WINDOWS)
    row0 = pl.program_id(1) * ts - HALO

    xh_scr[0:HALO, :] = xp_ref[0].astype(jnp.bfloat16)
    xh_scr[HALO:HALO + ts, :] = x_ref[0].astype(jnp.bfloat16)
    xh_scr[HALO + ts:rows, :] = xn_ref[0].astype(jnp.bfloat16)

    def valid_rows(width):
        r = lax.broadcasted_iota(jnp.int32, (rows, width), 0) + row0
        return (r >= 0) & (r < seq_len)

    valid_c = valid_rows(MXU_COLS)
    for c in range(n_chunk):
        lo = c * MXU_COLS
        up = _dot(xh_scr[...], w_in_ref[:, lo:lo + MXU_COLS]) + b_in_ref[:, lo:lo + MXU_COLS]
        up = jnp.where(valid_c, up, 0.0)
        for s in range(MXU_COLS // LANES):
            up_scr[c * (MXU_COLS // LANES) + s] = up[:, s * LANES:(s + 1) * LANES]

    conv_off = d + 0
    gate_off = 2 * d
    for c in range(n_chunk):
        lo = c * MXU_COLS
        a = _dot(xh_scr[...], w_in_ref[:, conv_off + lo:conv_off + lo + MXU_COLS])
        a = a + b_in_ref[:, conv_off + lo:conv_off + lo + MXU_COLS]
        gt = _dot(xh_scr[...], w_in_ref[:, gate_off + lo:gate_off + lo + MXU_COLS])
        gt = gt + b_in_ref[:, gate_off + lo:gate_off + lo + MXU_COLS]
        h = jnp.where(valid_c, a * jax.nn.sigmoid(gt), 0.0)
        for s in range(MXU_COLS // LANES):
            h_scr[c * (MXU_COLS // LANES) + s] = h[:, s * LANES:(s + 1) * LANES]
    pad = (CONV_KERNEL - 1) // 2
    gp_off = 3 * d
    gc_off = 4 * d

    def dep_zero(v):
        bits = lax.bitcast_convert_type(v, jnp.uint32)
        zero = lax.shift_right_logical(lax.shift_right_logical(bits, jnp.uint32(16)), jnp.uint32(16))
        return lax.bitcast_convert_type(zero, jnp.float32)

    first_gated_slab = 2
    xm = xh_scr[HALO:HALO + ts, :]
    for j in range(n_slab):
        cw = conv_w_ref[:, j * LANES:(j + 1) * LANES]
        c = j - first_gated_slab
        if 0 <= c < n_chunk:
            lo = c * MXU_COLS
            ugp = _dot(xm, w_in_ref[:, gp_off + lo:gp_off + lo + MXU_COLS]) + b_in_ref[:, gp_off + lo:gp_off + lo + MXU_COLS]
            ugc = _dot(xm, w_in_ref[:, gc_off + lo:gc_off + lo + MXU_COLS]) + b_in_ref[:, gc_off + lo:gc_off + lo + MXU_COLS]
            gate_p = jax.nn.sigmoid(ugp)
            gate_c = jax.nn.sigmoid(ugc)
            gp_scr[:, lo:lo + MXU_COLS] = gate_p
            gc_scr[:, lo:lo + MXU_COLS] = gate_c
            cw = cw + dep_zero(gate_p[0:1, 0:LANES] + gate_c[0:1, 0:LANES])
        acc = h_scr[j, pl.ds(HALO - pad, ts), :] * cw[0:1, :]
        for k in range(1, CONV_KERNEL):
            acc = acc + h_scr[j, pl.ds(HALO - pad + k, ts), :] * cw[k:k + 1, :]
        hc_scr[:, j * LANES:(j + 1) * LANES] = acc + conv_b_ref[:, j * LANES:(j + 1) * LANES]

    t = lax.broadcasted_iota(jnp.int32, (ts, LANES), 0) + (row0 + HALO)
    for g, win in enumerate(POOL_WINDOWS):
        half = win // 2
        cnt = jnp.minimum(t + half, seq_len) - jnp.maximum(t - half, 0)
        inv_cnt = 1.0 / cnt.astype(jnp.float32)
        for s in range(group_dim // LANES):
            j = g * (group_dim // LANES) + s
            wsum = up_scr[j, pl.ds(HALO - half, ts), :]
            for k in range(1, win):
                wsum = wsum + up_scr[j, pl.ds(HALO - half + k, ts), :]
            pooled = wsum * inv_cnt - up_scr[j, pl.ds(HALO, ts), :]
            pooled_scr[:, j * LANES:(j + 1) * LANES] = pooled.astype(jnp.bfloat16)
    for g in range(len(POOL_WINDOWS)):
        lo = g * group_dim
        pg = _dot(pooled_scr[:, lo:lo + group_dim], pool_w_ref[g])
        pw_scr[:, lo:lo + group_dim] = (pg * pool_scale_ref[:, lo:lo + group_dim]).astype(jnp.bfloat16)
    y_pool = _dot(pw_scr[...], w_pp_ref[...])

    for r in range(ts // LN_ROWS):
        blk = hc_scr[r * LN_ROWS:(r + 1) * LN_ROWS, :]
        y = _layer_norm(blk, cln_g_ref[...], cln_b_ref[...])
        half_y = (0.5 * y).astype(jnp.bfloat16)
        hs_scr[r * LN_ROWS:(r + 1) * LN_ROWS, :] = half_y + half_y * jnp.tanh(half_y)
    y_conv = _dot(hs_scr[...], w_cp_ref[...])

    for c in range(n_chunk):
        lo = c * MXU_COLS
        merged = (gp_scr[:, lo:lo + MXU_COLS] * y_pool[:, lo:lo + MXU_COLS]
                  + gc_scr[:, lo:lo + MXU_COLS] * y_conv[:, lo:lo + MXU_COLS])
        mg_scr[:, lo:lo + MXU_COLS] = merged.astype(jnp.bfloat16)
    mix = _dot(mg_scr[...], w_out_ref[...])
    for r in range(ts // LN_ROWS):
        sl = slice(r * LN_ROWS, (r + 1) * LN_ROWS)
        res = alpha * x_ref[0, sl, :] + mix[sl, :]
        o_ref[0, sl, :] = _layer_norm(res, ln_g_ref[...], ln_b_ref[...])


def _ffn_kernel(x_ref, w_in_ref, w_down_ref, ln_g_ref, ln_b_ref, o_ref, xb_scr, hid_scr, *, alpha):
    tm = x_ref.shape[0]
    d_ff = w_down_ref.shape[0]
    xb_scr[...] = x_ref[...].astype(jnp.bfloat16)
    for c in range(d_ff // MXU_COLS):
        lo = c * MXU_COLS
        g = _dot(xb_scr[...], w_in_ref[:, lo:lo + MXU_COLS])
        u = _dot(xb_scr[...], w_in_ref[:, d_ff + lo:d_ff + lo + MXU_COLS])
        hid_scr[:, lo:lo + MXU_COLS] = (g * jax.nn.sigmoid(g) * u).astype(jnp.bfloat16)
    ffn = _dot(hid_scr[...], w_down_ref[...])
    for r in range(tm // LN_ROWS):
        sl = slice(r * LN_ROWS, (r + 1) * LN_ROWS)
        res = alpha * x_ref[sl, :] + ffn[sl, :]
        o_ref[sl, :] = _layer_norm(res, ln_g_ref[...], ln_b_ref[...])


def _resident(shape):
    zeros = (0,) * len(shape)
    return pl.BlockSpec(shape, lambda *_: zeros, pipeline_mode=pl.Buffered(1))


def _token_mix(x, w_in, b_in, pool_w, pool_scale, w_pp, conv_w, conv_b, cln_g, cln_b, w_cp,
               w_out, ln_g, ln_b, alpha):
    b, s, d = x.shape
    ts = SEQ_TILE
    assert s % ts == 0 and ts % HALO == 0 and d % MXU_COLS == 0
    assert d // len(POOL_WINDOWS) % LANES == 0
    rows = ts + 2 * HALO
    halo_blocks = ts // HALO
    last_halo_block = s // HALO - 1
    residents = (w_in, b_in, pool_w, pool_scale, w_pp, conv_w, conv_b, cln_g, cln_b, w_cp, w_out, ln_g, ln_b)
    return pl.pallas_call(
        functools.partial(_mix_kernel, seq_len=s, alpha=alpha),
        out_shape=jax.ShapeDtypeStruct((b, s, d), jnp.float32),
        grid=(b, s // ts),
        in_specs=[
            pl.BlockSpec((1, ts, d), lambda bi, i: (bi, i, 0)),
            pl.BlockSpec((1, HALO, d), lambda bi, i: (bi, jnp.maximum(i * halo_blocks - 1, 0), 0)),
            pl.BlockSpec((1, HALO, d), lambda bi, i: (bi, jnp.minimum((i + 1) * halo_blocks, last_halo_block), 0)),
        ] + [_resident(a.shape) for a in residents],
        out_specs=pl.BlockSpec((1, ts, d), lambda bi, i: (bi, i, 0)),
        scratch_shapes=[
            pltpu.VMEM((rows, d), jnp.bfloat16),
            pltpu.VMEM((d // LANES, rows, LANES), jnp.float32),
            pltpu.VMEM((d // LANES, rows, LANES), jnp.float32),
            pltpu.VMEM((ts, d), jnp.bfloat16),
            pltpu.VMEM((ts, d), jnp.bfloat16),
            pltpu.VMEM((ts, d), jnp.float32),
            pltpu.VMEM((ts, d), jnp.bfloat16),
            pltpu.VMEM((ts, d), jnp.bfloat16),
            pltpu.VMEM((ts, d), jnp.float32),
            pltpu.VMEM((ts, d), jnp.float32),
        ],
        compiler_params=pltpu.CompilerParams(
            dimension_semantics=("arbitrary", "arbitrary"),
            vmem_limit_bytes=VMEM_LIMIT_BYTES),
        name="token_mix",
    )(x, x, x, *residents)


def _ffn(x2d, w_in, w_down, ln_g, ln_b, alpha):
    t, d = x2d.shape
    tm = ROW_TILE
    d_ff = w_down.shape[0]
    assert t % tm == 0 and d_ff % MXU_COLS == 0
    residents = (w_in, w_down, ln_g, ln_b)
    return pl.pallas_call(
        functools.partial(_ffn_kernel, alpha=alpha),
        out_shape=jax.ShapeDtypeStruct((t, d), jnp.float32),
        grid=(t // tm,),
        in_specs=[pl.BlockSpec((tm, d), lambda i: (i, 0))] + [_resident(a.shape) for a in residents],
        out_specs=pl.BlockSpec((tm, d), lambda i: (i, 0)),
        scratch_shapes=[
            pltpu.VMEM((tm, d), jnp.bfloat16),
            pltpu.VMEM((tm, d_ff), jnp.bfloat16),
        ],
        compiler_params=pltpu.CompilerParams(
            dimension_semantics=("arbitrary",),
            vmem_limit_bytes=VMEM_LIMIT_BYTES),
        name="swiglu_ffn",
    )(x2d, *residents)


def kernel(x, w_in, b_in, pool_w, pool_scale, w_pool_proj, conv_w, conv_b, conv_ln_g, conv_ln_b, w_conv_proj, w_out, ln1_g, ln1_b, w_ffn_in, w_ffn_down, ln2_g, ln2_b):
    b, s, d = x.shape
    depth = w_in.shape[0]
    alpha = (2.0 * depth) ** 0.25
    bf = lambda w: w.astype(jnp.bfloat16)
    row = lambda v: v.reshape(1, -1)
    for l in range(depth):
        x = _token_mix(
            x, bf(w_in[l]), row(b_in[l]), bf(pool_w[l]), row(pool_scale[l]), bf(w_pool_proj[l]),
            conv_w[l], row(conv_b[l]), row(conv_ln_g[l]), row(conv_ln_b[l]), bf(w_conv_proj[l]),
            bf(w_out[l]), row(ln1_g[l]), row(ln1_b[l]), alpha)
        x = _ffn(x.reshape(b * s, d), bf(w_ffn_in[l]), bf(w_ffn_down[l]), row(ln2_g[l]), row(ln2_b[l]),
                 alpha).reshape(b, s, d)
    return x
```

```python
import functools

import jax
import jax.numpy as jnp
from jax import lax
from jax.experimental import pallas as pl
from jax.experimental.pallas import tpu as pltpu

POOL_WINDOWS = (2, 4, 8, 16)
CONV_KERNEL = 31
LN_EPS = 1e-5

LANES = 128
MXU_COLS = 256
HALO = 16
SEQ_TILE = 512
ROW_TILE = 1024
LN_ROWS = 16
VMEM_LIMIT_BYTES = 56 * 1024 * 1024


def _layer_norm(v, g, b):
    mu = jnp.mean(v, axis=-1, keepdims=True)
    d = v - mu
    var = jnp.mean(d * d, axis=-1, keepdims=True)
    return d * lax.rsqrt(var + LN_EPS) * g + b


def _dot(a, b):
    return jnp.dot(a, b, preferred_element_type=jnp.float32)


def _mix_kernel(x_ref, xp_ref, xn_ref, w_in_ref, b_in_ref, pool_w_ref, pool_scale_ref,
                w_pp_ref, conv_w_ref, conv_b_ref, cln_g_ref, cln_b_ref, w_cp_ref,
                w_out_ref, ln_g_ref, ln_b_ref, o_ref,
                xh_scr, up_scr, h_scr, pooled_scr, pw_scr, hc_scr, hs_scr, mg_scr, gp_scr, gc_scr,
                *, seq_len, alpha):
    ts = x_ref.shape[1]
    d = x_ref.shape[2]
    rows = ts + 2 * HALO
    n_slab = d // LANES
    n_chunk = d // MXU_COLS
    group_dim = d // len(POOL_WINDOWS)
    row0 = pl.program_id(1) * ts - HALO

    xh_scr[0:HALO, :] = xp_ref[0].astype(jnp.bfloat16)
    xh_scr[HALO:HALO + ts, :] = x_ref[0].astype(jnp.bfloat16)
    xh_scr[HALO + ts:rows, :] = xn_ref[0].astype(jnp.bfloat16)

    def valid_rows(width):
        r = lax.broadcasted_iota(jnp.int32, (rows, width), 0) + row0
        return (r >= 0) & (r < seq_len)

    valid_c = valid_rows(MXU_COLS)
    for c in range(n_chunk):
        lo = c * MXU_COLS
        up = _dot(xh_scr[...], w_in_ref[:, lo:lo + MXU_COLS]) + b_in_ref[:, lo:lo + MXU_COLS]
        up = jnp.where(valid_c, up, 0.0)
        for s in range(MXU_COLS // LANES):
            up_scr[c * (MXU_COLS // LANES) + s] = up[:, s * LANES:(s + 1) * LANES]

    conv_off = d + 0
    gate_off = 2 * d
    for c in range(n_chunk):
        lo = c * MXU_COLS
        a = _dot(xh_scr[...], w_in_ref[:, conv_off + lo:conv_off + lo + MXU_COLS])
        a = a + b_in_ref[:, conv_off + lo:conv_off + lo + MXU_COLS]
        gt = _dot(xh_scr[...], w_in_ref[:, gate_off + lo:gate_off + lo + MXU_COLS])
        gt = gt + b_in_ref[:, gate_off + lo:gate_off + lo + MXU_COLS]
        h = jnp.where(valid_c, a * jax.nn.sigmoid(gt), 0.0)
        for s in range(MXU_COLS // LANES):
            h_scr[c * (MXU_COLS // LANES) + s] = h[:, s * LANES:(s + 1) * LANES]
    pad = (CONV_KERNEL - 1) // 2
    gp_off = 3 * d
    gc_off = 4 * d

    def dep_zero(v):
        bits = lax.bitcast_convert_type(v, jnp.uint32)
        zero = lax.shift_right_logical(lax.shift_right_logical(bits, jnp.uint32(16)), jnp.uint32(16))
        return lax.bitcast_convert_type(zero, jnp.float32)

    def pool_branch():
        t = lax.broadcasted_iota(jnp.int32, (ts, LANES), 0) + (row0 + HALO)
        for g, win in enumerate(POOL_WINDOWS):
            half = win // 2
            cnt = jnp.minimum(t + half, seq_len) - jnp.maximum(t - half, 0)
            inv_cnt = 1.0 / cnt.astype(jnp.float32)
            for s in range(group_dim // LANES):
                slab = g * (group_dim // LANES) + s
                wsum = up_scr[slab, pl.ds(HALO - half, ts), :]
                for k in range(1, win):
                    wsum = wsum + up_scr[slab, pl.ds(HALO - half + k, ts), :]
                pooled = wsum * inv_cnt - up_scr[slab, pl.ds(HALO, ts), :]
                pooled_scr[:, slab * LANES:(slab + 1) * LANES] = pooled.astype(jnp.bfloat16)
        for g in range(len(POOL_WINDOWS)):
            lo = g * group_dim
            pg = _dot(pooled_scr[:, lo:lo + group_dim], pool_w_ref[g])
            pw_scr[:, lo:lo + group_dim] = (pg * pool_scale_ref[:, lo:lo + group_dim]).astype(jnp.bfloat16)
        return _dot(pw_scr[...], w_pp_ref[...])

    first_gated_slab = 2
    pool_anchor_slab = first_gated_slab + n_chunk
    y_pool = None
    xm = xh_scr[HALO:HALO + ts, :]
    for j in range(n_slab):
        cw = conv_w_ref[:, j * LANES:(j + 1) * LANES]
        c = j - first_gated_slab
        if 0 <= c < n_chunk:
            lo = c * MXU_COLS
            ugp = _dot(xm, w_in_ref[:, gp_off + lo:gp_off + lo + MXU_COLS]) + b_in_ref[:, gp_off + lo:gp_off + lo + MXU_COLS]
            ugc = _dot(xm, w_in_ref[:, gc_off + lo:gc_off + lo + MXU_COLS]) + b_in_ref[:, gc_off + lo:gc_off + lo + MXU_COLS]
            gate_p = jax.nn.sigmoid(ugp)
            gate_c = jax.nn.sigmoid(ugc)
            gp_scr[:, lo:lo + MXU_COLS] = gate_p
            gc_scr[:, lo:lo + MXU_COLS] = gate_c
            cw = cw + dep_zero(gate_p[0:1, 0:LANES] + gate_c[0:1, 0:LANES])
        if j == pool_anchor_slab:
            y_pool = pool_branch()
            cw = cw + dep_zero(y_pool[0:1, 0:LANES])
        acc = h_scr[j, pl.ds(HALO - pad, ts), :] * cw[0:1, :]
        for k in range(1, CONV_KERNEL):
            acc = acc + h_scr[j, pl.ds(HALO - pad + k, ts), :] * cw[k:k + 1, :]
        hc_scr[:, j * LANES:(j + 1) * LANES] = acc + conv_b_ref[:, j * LANES:(j + 1) * LANES]

    for r in range(ts // LN_ROWS):
        blk = hc_scr[r * LN_ROWS:(r + 1) * LN_ROWS, :]
        y = _layer_norm(blk, cln_g_ref[...], cln_b_ref[...])
        half_y = (0.5 * y).astype(jnp.bfloat16)
        hs_scr[r * LN_ROWS:(r + 1) * LN_ROWS, :] = half_y + half_y * jnp.tanh(half_y)
    y_conv = _dot(hs_scr[...], w_cp_ref[...])

    for c in range(n_chunk):
        lo = c * MXU_COLS
        merged = (gp_scr[:, lo:lo + MXU_COLS] * y_pool[:, lo:lo + MXU_COLS]
                  + gc_scr[:, lo:lo + MXU_COLS] * y_conv[:, lo:lo + MXU_COLS])
        mg_scr[:, lo:lo + MXU_COLS] = merged.astype(jnp.bfloat16)
    mix = _dot(mg_scr[...], w_out_ref[...])
    for r in range(ts // LN_ROWS):
        sl = slice(r * LN_ROWS, (r + 1) * LN_ROWS)
        res = alpha * x_ref[0, sl, :] + mix[sl, :]
        o_ref[0, sl, :] = _layer_norm(res, ln_g_ref[...], ln_b_ref[...])


def _ffn_kernel(x_ref, w_in_ref, w_down_ref, ln_g_ref, ln_b_ref, o_ref, xb_scr, hid_scr, *, alpha):
    tm = x_ref.shape[0]
    d_ff = w_down_ref.shape[0]
    xb_scr[...] = x_ref[...].astype(jnp.bfloat16)
    for c in range(d_ff // MXU_COLS):
        lo = c * MXU_COLS
        g = _dot(xb_scr[...], w_in_ref[:, lo:lo + MXU_COLS])
        u = _dot(xb_scr[...], w_in_ref[:, d_ff + lo:d_ff + lo + MXU_COLS])
        hid_scr[:, lo:lo + MXU_COLS] = (g * jax.nn.sigmoid(g) * u).astype(jnp.bfloat16)
    ffn = _dot(hid_scr[...], w_down_ref[...])
    for r in range(tm // LN_ROWS):
        sl = slice(r * LN_ROWS, (r + 1) * LN_ROWS)
        res = alpha * x_ref[sl, :] + ffn[sl, :]
        o_ref[sl, :] = _layer_norm(res, ln_g_ref[...], ln_b_ref[...])


def _resident(shape):
    zeros = (0,) * len(shape)
    return pl.BlockSpec(shape, lambda *_: zeros, pipeline_mode=pl.Buffered(1))


def _token_mix(x, w_in, b_in, pool_w, pool_scale, w_pp, conv_w, conv_b, cln_g, cln_b, w_cp,
               w_out, ln_g, ln_b, alpha):
    b, s, d = x.shape
    ts = SEQ_TILE
    assert s % ts == 0 and ts % HALO == 0 and d % MXU_COLS == 0
    assert d // len(POOL_WINDOWS) % LANES == 0
    rows = ts + 2 * HALO
    halo_blocks = ts // HALO
    last_halo_block = s // HALO - 1
    residents = (w_in, b_in, pool_w, pool_scale, w_pp, conv_w, conv_b, cln_g, cln_b, w_cp, w_out, ln_g, ln_b)
    return pl.pallas_call(
        functools.partial(_mix_kernel, seq_len=s, alpha=alpha),
        out_shape=jax.ShapeDtypeStruct((b, s, d), jnp.float32),
        grid=(b, s // ts),
        in_specs=[
            pl.BlockSpec((1, ts, d), lambda bi, i: (bi, i, 0)),
            pl.BlockSpec((1, HALO, d), lambda bi, i: (bi, jnp.maximum(i * halo_blocks - 1, 0), 0)),
            pl.BlockSpec((1, HALO, d), lambda bi, i: (bi, jnp.minimum((i + 1) * halo_blocks, last_halo_block), 0)),
        ] + [_resident(a.shape) for a in residents],
        out_specs=pl.BlockSpec((1, ts, d), lambda bi, i: (bi, i, 0)),
        scratch_shapes=[
            pltpu.VMEM((rows, d), jnp.bfloat16),
            pltpu.VMEM((d // LANES, rows, LANES), jnp.float32),
            pltpu.VMEM((d // LANES, rows, LANES), jnp.float32),
            pltpu.VMEM((ts, d), jnp.bfloat16),
            pltpu.VMEM((ts, d), jnp.bfloat16),
            pltpu.VMEM((ts, d), jnp.float32),
            pltpu.VMEM((ts, d), jnp.bfloat16),
            pltpu.VMEM((ts, d), jnp.bfloat16),
            pltpu.VMEM((ts, d), jnp.float32),
            pltpu.VMEM((ts, d), jnp.float32),
        ],
        compiler_params=pltpu.CompilerParams(
            dimension_semantics=("arbitrary", "arbitrary"),
            vmem_limit_bytes=VMEM_LIMIT_BYTES),
        name="token_mix",
    )(x, x, x, *residents)


def _ffn(x2d, w_in, w_down, ln_g, ln_b, alpha):
    t, d = x2d.shape
    tm = ROW_TILE
    d_ff = w_down.shape[0]
    assert t % tm == 0 and d_ff % MXU_COLS == 0
    residents = (w_in, w_down, ln_g, ln_b)
    return pl.pallas_call(
        functools.partial(_ffn_kernel, alpha=alpha),
        out_shape=jax.ShapeDtypeStruct((t, d), jnp.float32),
        grid=(t // tm,),
        in_specs=[pl.BlockSpec((tm, d), lambda i: (i, 0))] + [_resident(a.shape) for a in residents],
        out_specs=pl.BlockSpec((tm, d), lambda i: (i, 0)),
        scratch_shapes=[
            pltpu.VMEM((tm, d), jnp.bfloat16),
            pltpu.VMEM((tm, d_ff), jnp.bfloat16),
        ],
        compiler_params=pltpu.CompilerParams(
            dimension_semantics=("arbitrary",),
            vmem_limit_bytes=VMEM_LIMIT_BYTES),
        name="swiglu_ffn",
    )(x2d, *residents)


def kernel(x, w_in, b_in, pool_w, pool_scale, w_pool_proj, conv_w, conv_b, conv_ln_g, conv_ln_b, w_conv_proj, w_out, ln1_g, ln1_b, w_ffn_in, w_ffn_down, ln2_g, ln2_b):
    b, s, d = x.shape
    depth = w_in.shape[0]
    alpha = (2.0 * depth) ** 0.25
    bf = lambda w: w.astype(jnp.bfloat16)
    row = lambda v: v.reshape(1, -1)
    for l in range(depth):
        x = _token_mix(
            x, bf(w_in[l]), row(b_in[l]), bf(pool_w[l]), row(pool_scale[l]), bf(w_pool_proj[l]),
            conv_w[l], row(conv_b[l]), row(conv_ln_g[l]), row(conv_ln_b[l]), bf(w_conv_proj[l]),
            bf(w_out[l]), row(ln1_g[l]), row(ln1_b[l]), alpha)
        x = _ffn(x.reshape(b * s, d), bf(w_ffn_in[l]), bf(w_ffn_down[l]), row(ln2_g[l]), row(ln2_b[l]),
                 alpha).reshape(b, s, d)
    return x
```

```python
import functools

import jax
import jax.numpy as jnp
from jax import lax
from jax.experimental import pallas as pl
from jax.experimental.pallas import tpu as pltpu

POOL_WINDOWS = (2, 4, 8, 16)
CONV_KERNEL = 31
LN_EPS = 1e-5

LANES = 128
MXU_COLS = 256
HALO = 16
SEQ_TILE = 512
ROW_TILE = 1024
LN_ROWS = 16
TAIL_PIECES = 2
VMEM_LIMIT_BYTES = 56 * 1024 * 1024


def _layer_norm(v, g, b):
    mu = jnp.mean(v, axis=-1, keepdims=True)
    d = v - mu
    var = jnp.mean(d * d, axis=-1, keepdims=True)
    return d * lax.rsqrt(var + LN_EPS) * g + b


def _dot(a, b):
    return jnp.dot(a, b, preferred_element_type=jnp.float32)


def _mix_kernel(x_ref, xp_ref, xn_ref, w_in_ref, b_in_ref, pool_w_ref, pool_scale_ref,
                w_pp_ref, conv_w_ref, conv_b_ref, cln_g_ref, cln_b_ref, w_cp_ref,
                w_out_ref, ln_g_ref, ln_b_ref, o_ref,
                xh_scr, up_scr, h_scr, pooled_scr, pw_scr, hc_scr, hs_scr, mg_scr, gp_scr, gc_scr,
                *, seq_len, alpha):
    ts = x_ref.shape[1]
    d = x_ref.shape[2]
    rows = ts + 2 * HALO
    n_slab = d // LANES
    n_chunk = d // MXU_COLS
    group_dim = d // len(POOL_WINDOWS)
    row0 = pl.program_id(1) * ts - HALO

    xh_scr[0:HALO, :] = xp_ref[0].astype(jnp.bfloat16)
    xh_scr[HALO:HALO + ts, :] = x_ref[0].astype(jnp.bfloat16)
    xh_scr[HALO + ts:rows, :] = xn_ref[0].astype(jnp.bfloat16)

    def valid_rows(width):
        r = lax.broadcasted_iota(jnp.int32, (rows, width), 0) + row0
        return (r >= 0) & (r < seq_len)

    valid_c = valid_rows(MXU_COLS)
    for c in range(n_chunk):
        lo = c * MXU_COLS
        up = _dot(xh_scr[...], w_in_ref[:, lo:lo + MXU_COLS]) + b_in_ref[:, lo:lo + MXU_COLS]
        up = jnp.where(valid_c, up, 0.0)
        for s in range(MXU_COLS // LANES):
            up_scr[c * (MXU_COLS // LANES) + s] = up[:, s * LANES:(s + 1) * LANES]

    conv_off = d + 0
    gate_off = 2 * d
    for c in range(n_chunk):
        lo = c * MXU_COLS
        a = _dot(xh_scr[...], w_in_ref[:, conv_off + lo:conv_off + lo + MXU_COLS])
        a = a + b_in_ref[:, conv_off + lo:conv_off + lo + MXU_COLS]
        gt = _dot(xh_scr[...], w_in_ref[:, gate_off + lo:gate_off + lo + MXU_COLS])
        gt = gt + b_in_ref[:, gate_off + lo:gate_off + lo + MXU_COLS]
        h = jnp.where(valid_c, a * jax.nn.sigmoid(gt), 0.0)
        for s in range(MXU_COLS // LANES):
            h_scr[c * (MXU_COLS // LANES) + s] = h[:, s * LANES:(s + 1) * LANES]
    pad = (CONV_KERNEL - 1) // 2
    gp_off = 3 * d
    gc_off = 4 * d

    def dep_zero(v):
        bits = lax.bitcast_convert_type(v, jnp.uint32)
        zero = lax.shift_right_logical(lax.shift_right_logical(bits, jnp.uint32(16)), jnp.uint32(16))
        return lax.bitcast_convert_type(zero, jnp.float32)

    def pool_branch():
        t = lax.broadcasted_iota(jnp.int32, (ts, LANES), 0) + (row0 + HALO)
        for g, win in enumerate(POOL_WINDOWS):
            half = win // 2
            cnt = jnp.minimum(t + half, seq_len) - jnp.maximum(t - half, 0)
            inv_cnt = 1.0 / cnt.astype(jnp.float32)
            for s in range(group_dim // LANES):
                slab = g * (group_dim // LANES) + s
                wsum = up_scr[slab, pl.ds(HALO - half, ts), :]
                for k in range(1, win):
                    wsum = wsum + up_scr[slab, pl.ds(HALO - half + k, ts), :]
                pooled = wsum * inv_cnt - up_scr[slab, pl.ds(HALO, ts), :]
                pooled_scr[:, slab * LANES:(slab + 1) * LANES] = pooled.astype(jnp.bfloat16)
        for g in range(len(POOL_WINDOWS)):
            lo = g * group_dim
            pg = _dot(pooled_scr[:, lo:lo + group_dim], pool_w_ref[g])
            pw_scr[:, lo:lo + group_dim] = (pg * pool_scale_ref[:, lo:lo + group_dim]).astype(jnp.bfloat16)
        return _dot(pw_scr[...], w_pp_ref[...])

    first_gated_slab = 2
    pool_anchor_slab = first_gated_slab + n_chunk
    y_pool = None
    xm = xh_scr[HALO:HALO + ts, :]
    for j in range(n_slab):
        cw = conv_w_ref[:, j * LANES:(j + 1) * LANES]
        c = j - first_gated_slab
        if 0 <= c < n_chunk:
            lo = c * MXU_COLS
            ugp = _dot(xm, w_in_ref[:, gp_off + lo:gp_off + lo + MXU_COLS]) + b_in_ref[:, gp_off + lo:gp_off + lo + MXU_COLS]
            ugc = _dot(xm, w_in_ref[:, gc_off + lo:gc_off + lo + MXU_COLS]) + b_in_ref[:, gc_off + lo:gc_off + lo + MXU_COLS]
            gate_p = jax.nn.sigmoid(ugp)
            gate_c = jax.nn.sigmoid(ugc)
            gp_scr[:, lo:lo + MXU_COLS] = gate_p
            gc_scr[:, lo:lo + MXU_COLS] = gate_c
            cw = cw + dep_zero(gate_p[0:1, 0:LANES] + gate_c[0:1, 0:LANES])
        if j == pool_anchor_slab:
            y_pool = pool_branch()
            cw = cw + dep_zero(y_pool[0:1, 0:LANES])
        acc = h_scr[j, pl.ds(HALO - pad, ts), :] * cw[0:1, :]
        for k in range(1, CONV_KERNEL):
            acc = acc + h_scr[j, pl.ds(HALO - pad + k, ts), :] * cw[k:k + 1, :]
        hc_scr[:, j * LANES:(j + 1) * LANES] = acc + conv_b_ref[:, j * LANES:(j + 1) * LANES]

    piece = ts // TAIL_PIECES
    for p in range(TAIL_PIECES):
        rows_p = slice(p * piece, (p + 1) * piece)
        for r in range(piece // LN_ROWS):
            sl = slice(p * piece + r * LN_ROWS, p * piece + (r + 1) * LN_ROWS)
            y = _layer_norm(hc_scr[sl, :], cln_g_ref[...], cln_b_ref[...])
            half_y = (0.5 * y).astype(jnp.bfloat16)
            hs_scr[sl, :] = half_y + half_y * jnp.tanh(half_y)
        y_conv = _dot(hs_scr[rows_p, :], w_cp_ref[...])
        for c in range(n_chunk):
            cols = slice(c * MXU_COLS, (c + 1) * MXU_COLS)
            merged = gp_scr[rows_p, cols] * y_pool[rows_p, cols] + gc_scr[rows_p, cols] * y_conv[:, cols]
            mg_scr[rows_p, cols] = merged.astype(jnp.bfloat16)
        mix = _dot(mg_scr[rows_p, :], w_out_ref[...])
        for r in range(piece // LN_ROWS):
            sl = slice(p * piece + r * LN_ROWS, p * piece + (r + 1) * LN_ROWS)
            res = alpha * x_ref[0, sl, :] + mix[r * LN_ROWS:(r + 1) * LN_ROWS, :]
            o_ref[0, sl, :] = _layer_norm(res, ln_g_ref[...], ln_b_ref[...])


def _ffn_kernel(x_ref, w_in_ref, w_down_ref, ln_g_ref, ln_b_ref, o_ref, xb_scr, hid_scr, *, alpha):
    tm = x_ref.shape[0]
    d_ff = w_down_ref.shape[0]
    xb_scr[...] = x_ref[...].astype(jnp.bfloat16)
    for c in range(d_ff // MXU_COLS):
        lo = c * MXU_COLS
        g = _dot(xb_scr[...], w_in_ref[:, lo:lo + MXU_COLS])
        u = _dot(xb_scr[...], w_in_ref[:, d_ff + lo:d_ff + lo + MXU_COLS])
        hid_scr[:, lo:lo + MXU_COLS] = (g * jax.nn.sigmoid(g) * u).astype(jnp.bfloat16)
    piece = tm // TAIL_PIECES
    for p in range(TAIL_PIECES):
        ffn = _dot(hid_scr[p * piece:(p + 1) * piece, :], w_down_ref[...])
        for r in range(piece // LN_ROWS):
            sl = slice(p * piece + r * LN_ROWS, p * piece + (r + 1) * LN_ROWS)
            res = alpha * x_ref[sl, :] + ffn[r * LN_ROWS:(r + 1) * LN_ROWS, :]
            o_ref[sl, :] = _layer_norm(res, ln_g_ref[...], ln_b_ref[...])


def _resident(shape):
    zeros = (0,) * len(shape)
    return pl.BlockSpec(shape, lambda *_: zeros, pipeline_mode=pl.Buffered(1))


def _token_mix(x, w_in, b_in, pool_w, pool_scale, w_pp, conv_w, conv_b, cln_g, cln_b, w_cp,
               w_out, ln_g, ln_b, alpha):
    b, s, d = x.shape
    ts = SEQ_TILE
    assert s % ts == 0 and ts % HALO == 0 and d % MXU_COLS == 0
    assert d // len(POOL_WINDOWS) % LANES == 0
    rows = ts + 2 * HALO
    halo_blocks = ts // HALO
    last_halo_block = s // HALO - 1
    residents = (w_in, b_in, pool_w, pool_scale, w_pp, conv_w, conv_b, cln_g, cln_b, w_cp, w_out, ln_g, ln_b)
    return pl.pallas_call(
        functools.partial(_mix_kernel, seq_len=s, alpha=alpha),
        out_shape=jax.ShapeDtypeStruct((b, s, d), jnp.float32),
        grid=(b, s // ts),
        in_specs=[
            pl.BlockSpec((1, ts, d), lambda bi, i: (bi, i, 0)),
            pl.BlockSpec((1, HALO, d), lambda bi, i: (bi, jnp.maximum(i * halo_blocks - 1, 0), 0)),
            pl.BlockSpec((1, HALO, d), lambda bi, i: (bi, jnp.minimum((i + 1) * halo_blocks, last_halo_block), 0)),
        ] + [_resident(a.shape) for a in residents],
        out_specs=pl.BlockSpec((1, ts, d), lambda bi, i: (bi, i, 0)),
        scratch_shapes=[
            pltpu.VMEM((rows, d), jnp.bfloat16),
            pltpu.VMEM((d // LANES, rows, LANES), jnp.float32),
            pltpu.VMEM((d // LANES, rows, LANES), jnp.float32),
            pltpu.VMEM((ts, d), jnp.bfloat16),
            pltpu.VMEM((ts, d), jnp.bfloat16),
            pltpu.VMEM((ts, d), jnp.float32),
            pltpu.VMEM((ts, d), jnp.bfloat16),
            pltpu.VMEM((ts, d), jnp.bfloat16),
            pltpu.VMEM((ts, d), jnp.float32),
            pltpu.VMEM((ts, d), jnp.float32),
        ],
        compiler_params=pltpu.CompilerParams(
            dimension_semantics=("arbitrary", "arbitrary"),
            vmem_limit_bytes=VMEM_LIMIT_BYTES),
        name="token_mix",
    )(x, x, x, *residents)


def _ffn(x2d, w_in, w_down, ln_g, ln_b, alpha):
    t, d = x2d.shape
    tm = ROW_TILE
    d_ff = w_down.shape[0]
    assert t % tm == 0 and d_ff % MXU_COLS == 0
    residents = (w_in, w_down, ln_g, ln_b)
    return pl.pallas_call(
        functools.partial(_ffn_kernel, alpha=alpha),
        out_shape=jax.ShapeDtypeStruct((t, d), jnp.float32),
        grid=(t // tm,),
        in_specs=[pl.BlockSpec((tm, d), lambda i: (i, 0))] + [_resident(a.shape) for a in residents],
        out_specs=pl.BlockSpec((tm, d), lambda i: (i, 0)),
        scratch_shapes=[
            pltpu.VMEM((tm, d), jnp.bfloat16),
            pltpu.VMEM((tm, d_ff), jnp.bfloat16),
        ],
        compiler_params=pltpu.CompilerParams(
            dimension_semantics=("arbitrary",),
            vmem_limit_bytes=VMEM_LIMIT_BYTES),
        name="swiglu_ffn",
    )(x2d, *residents)


def kernel(x, w_in, b_in, pool_w, pool_scale, w_pool_proj, conv_w, conv_b, conv_ln_g, conv_ln_b, w_conv_proj, w_out, ln1_g, ln1_b, w_ffn_in, w_ffn_down, ln2_g, ln2_b):
    b, s, d = x.shape
    depth = w_in.shape[0]
    alpha = (2.0 * depth) ** 0.25
    bf = lambda w: w.astype(jnp.bfloat16)
    row = lambda v: v.reshape(1, -1)
    for l in range(depth):
        x = _token_mix(
            x, bf(w_in[l]), row(b_in[l]), bf(pool_w[l]), row(pool_scale[l]), bf(w_pool_proj[l]),
            conv_w[l], row(conv_b[l]), row(conv_ln_g[l]), row(conv_ln_b[l]), bf(w_conv_proj[l]),
            bf(w_out[l]), row(ln1_g[l]), row(ln1_b[l]), alpha)
        x = _ffn(x.reshape(b * s, d), bf(w_ffn_in[l]), bf(w_ffn_down[l]), row(ln2_g[l]), row(ln2_b[l]),
                 alpha).reshape(b, s, d)
    return x
```

```python
import functools

import jax
import jax.numpy as jnp
from jax import lax
from jax.experimental import pallas as pl
from jax.experimental.pallas import tpu as pltpu

POOL_WINDOWS = (2, 4, 8, 16)
CONV_KERNEL = 31
LN_EPS = 1e-5

LANES = 128
MXU_COLS = 256
HALO = 16
SEQ_TILE = 512
ROW_TILE = 1024
LN_ROWS = 16
TAIL_PIECES = 2
VMEM_LIMIT_BYTES = 56 * 1024 * 1024


def _layer_norm(v, g, b):
    mu = jnp.mean(v, axis=-1, keepdims=True)
    d = v - mu
    var = jnp.mean(d * d, axis=-1, keepdims=True)
    return d * lax.rsqrt(var + LN_EPS) * g + b


def _dot(a, b):
    return jnp.dot(a, b, preferred_element_type=jnp.float32)


def _mix_kernel(x_ref, xp_ref, xn_ref, w_in_ref, b_in_ref, pool_w_ref, pool_scale_ref,
                w_pp_ref, conv_w_ref, conv_b_ref, cln_g_ref, cln_b_ref, w_cp_ref,
                w_out_ref, ln_g_ref, ln_b_ref, o_ref,
                xh_scr, up_scr, h_scr, pooled_scr, pw_scr, hc_scr, hs_scr, mg_scr, gp_scr, gc_scr, ws_scr,
                *, seq_len, alpha):
    ts = x_ref.shape[1]
    d = x_ref.shape[2]
    rows = ts + 2 * HALO
    n_slab = d // LANES
    n_chunk = d // MXU_COLS
    group_dim = d // len(POOL_WINDOWS)
    row0 = pl.program_id(1) * ts - HALO

    xh_scr[0:HALO, :] = xp_ref[0].astype(jnp.bfloat16)
    xh_scr[HALO:HALO + ts, :] = x_ref[0].astype(jnp.bfloat16)
    xh_scr[HALO + ts:rows, :] = xn_ref[0].astype(jnp.bfloat16)

    def valid_rows(width):
        r = lax.broadcasted_iota(jnp.int32, (rows, width), 0) + row0
        return (r >= 0) & (r < seq_len)

    valid_c = valid_rows(MXU_COLS)
    for c in range(n_chunk):
        lo = c * MXU_COLS
        up = _dot(xh_scr[...], w_in_ref[:, lo:lo + MXU_COLS]) + b_in_ref[:, lo:lo + MXU_COLS]
        up = jnp.where(valid_c, up, 0.0)
        for s in range(MXU_COLS // LANES):
            up_scr[c * (MXU_COLS // LANES) + s] = up[:, s * LANES:(s + 1) * LANES]

    conv_off = d + 0
    gate_off = 2 * d
    for c in range(n_chunk):
        lo = c * MXU_COLS
        a = _dot(xh_scr[...], w_in_ref[:, conv_off + lo:conv_off + lo + MXU_COLS])
        a = a + b_in_ref[:, conv_off + lo:conv_off + lo + MXU_COLS]
        gt = _dot(xh_scr[...], w_in_ref[:, gate_off + lo:gate_off + lo + MXU_COLS])
        gt = gt + b_in_ref[:, gate_off + lo:gate_off + lo + MXU_COLS]
        h = jnp.where(valid_c, a * jax.nn.sigmoid(gt), 0.0)
        for s in range(MXU_COLS // LANES):
            h_scr[c * (MXU_COLS // LANES) + s] = h[:, s * LANES:(s + 1) * LANES]
    pad = (CONV_KERNEL - 1) // 2
    gp_off = 3 * d
    gc_off = 4 * d

    def dep_zero(v):
        bits = lax.bitcast_convert_type(v, jnp.uint32)
        zero = lax.shift_right_logical(lax.shift_right_logical(bits, jnp.uint32(16)), jnp.uint32(16))
        return lax.bitcast_convert_type(zero, jnp.float32)

    def window_sum(slab, win):
        base = HALO - win // 2
        span, src, buf = 1, up_scr.at[slab], 0
        while 2 * span < win:
            n = ts + win - 2 * span
            ws_scr[buf, pl.ds(base, n), :] = src[pl.ds(base, n), :] + src[pl.ds(base + span, n), :]
            span, src, buf = 2 * span, ws_scr.at[buf], 1 - buf
        return src[pl.ds(base, ts), :] + src[pl.ds(base + span, ts), :]

    def pool_branch():
        t = lax.broadcasted_iota(jnp.int32, (ts, LANES), 0) + (row0 + HALO)
        for g, win in enumerate(POOL_WINDOWS):
            half = win // 2
            cnt = jnp.minimum(t + half, seq_len) - jnp.maximum(t - half, 0)
            inv_cnt = 1.0 / cnt.astype(jnp.float32)
            for s in range(group_dim // LANES):
                slab = g * (group_dim // LANES) + s
                pooled = window_sum(slab, win) * inv_cnt - up_scr[slab, pl.ds(HALO, ts), :]
                pooled_scr[:, slab * LANES:(slab + 1) * LANES] = pooled.astype(jnp.bfloat16)
        for g in range(len(POOL_WINDOWS)):
            lo = g * group_dim
            pg = _dot(pooled_scr[:, lo:lo + group_dim], pool_w_ref[g])
            pw_scr[:, lo:lo + group_dim] = (pg * pool_scale_ref[:, lo:lo + group_dim]).astype(jnp.bfloat16)
        return _dot(pw_scr[...], w_pp_ref[...])

    first_gated_slab = 2
    pool_anchor_slab = first_gated_slab + n_chunk
    y_pool = None
    xm = xh_scr[HALO:HALO + ts, :]
    for j in range(n_slab):
        cw = conv_w_ref[:, j * LANES:(j + 1) * LANES]
        c = j - first_gated_slab
        if 0 <= c < n_chunk:
            lo = c * MXU_COLS
            ugp = _dot(xm, w_in_ref[:, gp_off + lo:gp_off + lo + MXU_COLS]) + b_in_ref[:, gp_off + lo:gp_off + lo + MXU_COLS]
            ugc = _dot(xm, w_in_ref[:, gc_off + lo:gc_off + lo + MXU_COLS]) + b_in_ref[:, gc_off + lo:gc_off + lo + MXU_COLS]
            gate_p = jax.nn.sigmoid(ugp)
            gate_c = jax.nn.sigmoid(ugc)
            gp_scr[:, lo:lo + MXU_COLS] = gate_p
            gc_scr[:, lo:lo + MXU_COLS] = gate_c
            cw = cw + dep_zero(gate_p[0:1, 0:LANES] + gate_c[0:1, 0:LANES])
        if j == pool_anchor_slab:
            y_pool = pool_branch()
            cw = cw + dep_zero(y_pool[0:1, 0:LANES])
        acc = h_scr[j, pl.ds(HALO - pad, ts), :] * cw[0:1, :]
        for k in range(1, CONV_KERNEL):
            acc = acc + h_scr[j, pl.ds(HALO - pad + k, ts), :] * cw[k:k + 1, :]
        hc_scr[:, j * LANES:(j + 1) * LANES] = acc + conv_b_ref[:, j * LANES:(j + 1) * LANES]

    piece = ts // TAIL_PIECES
    for p in range(TAIL_PIECES):
        rows_p = slice(p * piece, (p + 1) * piece)
        for r in range(piece // LN_ROWS):
            sl = slice(p * piece + r * LN_ROWS, p * piece + (r + 1) * LN_ROWS)
            y = _layer_norm(hc_scr[sl, :], cln_g_ref[...], cln_b_ref[...])
            half_y = (0.5 * y).astype(jnp.bfloat16)
            hs_scr[sl, :] = half_y + half_y * jnp.tanh(half_y)
        y_conv = _dot(hs_scr[rows_p, :], w_cp_ref[...])
        for c in range(n_chunk):
            cols = slice(c * MXU_COLS, (c + 1) * MXU_COLS)
            merged = gp_scr[rows_p, cols] * y_pool[rows_p, cols] + gc_scr[rows_p, cols] * y_conv[:, cols]
            mg_scr[rows_p, cols] = merged.astype(jnp.bfloat16)
        mix = _dot(mg_scr[rows_p, :], w_out_ref[...])
        for r in range(piece // LN_ROWS):
            sl = slice(p * piece + r * LN_ROWS, p * piece + (r + 1) * LN_ROWS)
            res = alpha * x_ref[0, sl, :] + mix[r * LN_ROWS:(r + 1) * LN_ROWS, :]
            o_ref[0, sl, :] = _layer_norm(res, ln_g_ref[...], ln_b_ref[...])


def _ffn_kernel(x_ref, w_in_ref, w_down_ref, ln_g_ref, ln_b_ref, o_ref, xb_scr, hid_scr, *, alpha):
    tm = x_ref.shape[0]
    d_ff = w_down_ref.shape[0]
    xb_scr[...] = x_ref[...].astype(jnp.bfloat16)
    for c in range(d_ff // MXU_COLS):
        lo = c * MXU_COLS
        g = _dot(xb_scr[...], w_in_ref[:, lo:lo + MXU_COLS])
        u = _dot(xb_scr[...], w_in_ref[:, d_ff + lo:d_ff + lo + MXU_COLS])
        hid_scr[:, lo:lo + MXU_COLS] = (g * jax.nn.sigmoid(g) * u).astype(jnp.bfloat16)
    piece = tm // TAIL_PIECES
    for p in range(TAIL_PIECES):
        ffn = _dot(hid_scr[p * piece:(p + 1) * piece, :], w_down_ref[...])
        for r in range(piece // LN_ROWS):
            sl = slice(p * piece + r * LN_ROWS, p * piece + (r + 1) * LN_ROWS)
            res = alpha * x_ref[sl, :] + ffn[r * LN_ROWS:(r + 1) * LN_ROWS, :]
            o_ref[sl, :] = _layer_norm(res, ln_g_ref[...], ln_b_ref[...])


def _resident(shape):
    zeros = (0,) * len(shape)
    return pl.BlockSpec(shape, lambda *_: zeros, pipeline_mode=pl.Buffered(1))


def _token_mix(x, w_in, b_in, pool_w, pool_scale, w_pp, conv_w, conv_b, cln_g, cln_b, w_cp,
               w_out, ln_g, ln_b, alpha):
    b, s, d = x.shape
    ts = SEQ_TILE
    assert s % ts == 0 and ts % HALO == 0 and d % MXU_COLS == 0
    assert d // len(POOL_WINDOWS) % LANES == 0
    rows = ts + 2 * HALO
    halo_blocks = ts // HALO
    last_halo_block = s // HALO - 1
    residents = (w_in, b_in, pool_w, pool_scale, w_pp, conv_w, conv_b, cln_g, cln_b, w_cp, w_out, ln_g, ln_b)
    return pl.pallas_call(
        functools.partial(_mix_kernel, seq_len=s, alpha=alpha),
        out_shape=jax.ShapeDtypeStruct((b, s, d), jnp.float32),
        grid=(b, s // ts),
        in_specs=[
            pl.BlockSpec((1, ts, d), lambda bi, i: (bi, i, 0)),
            pl.BlockSpec((1, HALO, d), lambda bi, i: (bi, jnp.maximum(i * halo_blocks - 1, 0), 0)),
            pl.BlockSpec((1, HALO, d), lambda bi, i: (bi, jnp.minimum((i + 1) * halo_blocks, last_halo_block), 0)),
        ] + [_resident(a.shape) for a in residents],
        out_specs=pl.BlockSpec((1, ts, d), lambda bi, i: (bi, i, 0)),
        scratch_shapes=[
            pltpu.VMEM((rows, d), jnp.bfloat16),
            pltpu.VMEM((d // LANES, rows, LANES), jnp.float32),
            pltpu.VMEM((d // LANES, rows, LANES), jnp.float32),
            pltpu.VMEM((ts, d), jnp.bfloat16),
            pltpu.VMEM((ts, d), jnp.bfloat16),
            pltpu.VMEM((ts, d), jnp.float32),
            pltpu.VMEM((ts, d), jnp.bfloat16),
            pltpu.VMEM((ts, d), jnp.bfloat16),
            pltpu.VMEM((ts, d), jnp.float32),
            pltpu.VMEM((ts, d), jnp.float32),
            pltpu.VMEM((2, rows, LANES), jnp.float32),
        ],
        compiler_params=pltpu.CompilerParams(
            dimension_semantics=("arbitrary", "arbitrary"),
            vmem_limit_bytes=VMEM_LIMIT_BYTES),
        name="token_mix",
    )(x, x, x, *residents)


def _ffn(x2d, w_in, w_down, ln_g, ln_b, alpha):
    t, d = x2d.shape
    tm = ROW_TILE
    d_ff = w_down.shape[0]
    assert t % tm == 0 and d_ff % MXU_COLS == 0
    residents = (w_in, w_down, ln_g, ln_b)
    return pl.pallas_call(
        functools.partial(_ffn_kernel, alpha=alpha),
        out_shape=jax.ShapeDtypeStruct((t, d), jnp.float32),
        grid=(t // tm,),
        in_specs=[pl.BlockSpec((tm, d), lambda i: (i, 0))] + [_resident(a.shape) for a in residents],
        out_specs=pl.BlockSpec((tm, d), lambda i: (i, 0)),
        scratch_shapes=[
            pltpu.VMEM((tm, d), jnp.bfloat16),
            pltpu.VMEM((tm, d_ff), jnp.bfloat16),
        ],
        compiler_params=pltpu.CompilerParams(
            dimension_semantics=("arbitrary",),
            vmem_limit_bytes=VMEM_LIMIT_BYTES),
        name="swiglu_ffn",
    )(x2d, *residents)


def kernel(x, w_in, b_in, pool_w, pool_scale, w_pool_proj, conv_w, conv_b, conv_ln_g, conv_ln_b, w_conv_proj, w_out, ln1_g, ln1_b, w_ffn_in, w_ffn_down, ln2_g, ln2_b):
    b, s, d = x.shape
    depth = w_in.shape[0]
    alpha = (2.0 * depth) ** 0.25
    bf = lambda w: w.astype(jnp.bfloat16)
    row = lambda v: v.reshape(1, -1)
    for l in range(depth):
        x = _token_mix(
            x, bf(w_in[l]), row(b_in[l]), bf(pool_w[l]), row(pool_scale[l]), bf(w_pool_proj[l]),
            conv_w[l], row(conv_b[l]), row(conv_ln_g[l]), row(conv_ln_b[l]), bf(w_conv_proj[l]),
            bf(w_out[l]), row(ln1_g[l]), row(ln1_b[l]), alpha)
        x = _ffn(x.reshape(b * s, d), bf(w_ffn_in[l]), bf(w_ffn_down[l]), row(ln2_g[l]), row(ln2_b[l]),
                 alpha).reshape(b, s, d)
    return x
```

```python
import functools

import jax
import jax.numpy as jnp
from jax import lax
from jax.experimental import pallas as pl
from jax.experimental.pallas import tpu as pltpu

POOL_WINDOWS = (2, 4, 8, 16)
CONV_KERNEL = 31
LN_EPS = 1e-5

LANES = 128
MXU_COLS = 256
HALO = 16
SEQ_TILE = 512
ROW_TILE = 1024
LN_ROWS = 16
TAIL_PIECES = 2
FFN_TAIL_PIECES = 4
VMEM_LIMIT_BYTES = 56 * 1024 * 1024


def _layer_norm(v, g, b):
    mu = jnp.mean(v, axis=-1, keepdims=True)
    d = v - mu
    var = jnp.mean(d * d, axis=-1, keepdims=True)
    return d * lax.rsqrt(var + LN_EPS) * g + b


def _dot(a, b):
    return jnp.dot(a, b, preferred_element_type=jnp.float32)


def _mix_kernel(x_ref, xp_ref, xn_ref, w_in_ref, b_in_ref, pool_w_ref, pool_scale_ref,
                w_pp_ref, conv_w_ref, conv_b_ref, cln_g_ref, cln_b_ref, w_cp_ref,
                w_out_ref, ln_g_ref, ln_b_ref, o_ref,
                xh_scr, up_scr, h_scr, pooled_scr, pw_scr, hc_scr, hs_scr, mg_scr, gp_scr, gc_scr,
                *, seq_len, alpha):
    ts = x_ref.shape[1]
    d = x_ref.shape[2]
    rows = ts + 2 * HALO
    n_slab = d // LANES
    n_chunk = d // MXU_COLS
    group_dim = d // len(POOL_WINDOWS)
    row0 = pl.program_id(1) * ts - HALO

    xh_scr[0:HALO, :] = xp_ref[0].astype(jnp.bfloat16)
    xh_scr[HALO:HALO + ts, :] = x_ref[0].astype(jnp.bfloat16)
    xh_scr[HALO + ts:rows, :] = xn_ref[0].astype(jnp.bfloat16)

    def valid_rows(width):
        r = lax.broadcasted_iota(jnp.int32, (rows, width), 0) + row0
        return (r >= 0) & (r < seq_len)

    valid_c = valid_rows(MXU_COLS)
    for c in range(n_chunk):
        lo = c * MXU_COLS
        up = _dot(xh_scr[...], w_in_ref[:, lo:lo + MXU_COLS]) + b_in_ref[:, lo:lo + MXU_COLS]
        up = jnp.where(valid_c, up, 0.0)
        for s in range(MXU_COLS // LANES):
            up_scr[c * (MXU_COLS // LANES) + s] = up[:, s * LANES:(s + 1) * LANES]

    conv_off = d + 0
    gate_off = 2 * d
    for c in range(n_chunk):
        lo = c * MXU_COLS
        a = _dot(xh_scr[...], w_in_ref[:, conv_off + lo:conv_off + lo + MXU_COLS])
        a = a + b_in_ref[:, conv_off + lo:conv_off + lo + MXU_COLS]
        gt = _dot(xh_scr[...], w_in_ref[:, gate_off + lo:gate_off + lo + MXU_COLS])
        gt = gt + b_in_ref[:, gate_off + lo:gate_off + lo + MXU_COLS]
        h = jnp.where(valid_c, a * jax.nn.sigmoid(gt), 0.0)
        for s in range(MXU_COLS // LANES):
            h_scr[c * (MXU_COLS // LANES) + s] = h[:, s * LANES:(s + 1) * LANES]
    pad = (CONV_KERNEL - 1) // 2
    gp_off = 3 * d
    gc_off = 4 * d

    def dep_zero(v):
        bits = lax.bitcast_convert_type(v, jnp.uint32)
        zero = lax.shift_right_logical(lax.shift_right_logical(bits, jnp.uint32(16)), jnp.uint32(16))
        return lax.bitcast_convert_type(zero, jnp.float32)

    def pool_branch():
        t = lax.broadcasted_iota(jnp.int32, (ts, LANES), 0) + (row0 + HALO)
        for g, win in enumerate(POOL_WINDOWS):
            half = win // 2
            cnt = jnp.minimum(t + half, seq_len) - jnp.maximum(t - half, 0)
            inv_cnt = 1.0 / cnt.astype(jnp.float32)
            for s in range(group_dim // LANES):
                slab = g * (group_dim // LANES) + s
                wsum = up_scr[slab, pl.ds(HALO - half, ts), :]
                for k in range(1, win):
                    wsum = wsum + up_scr[slab, pl.ds(HALO - half + k, ts), :]
                pooled = wsum * inv_cnt - up_scr[slab, pl.ds(HALO, ts), :]
                pooled_scr[:, slab * LANES:(slab + 1) * LANES] = pooled.astype(jnp.bfloat16)
        for g in range(len(POOL_WINDOWS)):
            lo = g * group_dim
            pg = _dot(pooled_scr[:, lo:lo + group_dim], pool_w_ref[g])
            pw_scr[:, lo:lo + group_dim] = (pg * pool_scale_ref[:, lo:lo + group_dim]).astype(jnp.bfloat16)
        return _dot(pw_scr[...], w_pp_ref[...])

    first_gated_slab = 2
    pool_anchor_slab = first_gated_slab + n_chunk
    y_pool = None
    xm = xh_scr[HALO:HALO + ts, :]
    for j in range(n_slab):
        cw = conv_w_ref[:, j * LANES:(j + 1) * LANES]
        c = j - first_gated_slab
        if 0 <= c < n_chunk:
            lo = c * MXU_COLS
            ugp = _dot(xm, w_in_ref[:, gp_off + lo:gp_off + lo + MXU_COLS]) + b_in_ref[:, gp_off + lo:gp_off + lo + MXU_COLS]
            ugc = _dot(xm, w_in_ref[:, gc_off + lo:gc_off + lo + MXU_COLS]) + b_in_ref[:, gc_off + lo:gc_off + lo + MXU_COLS]
            gate_p = jax.nn.sigmoid(ugp)
            gate_c = jax.nn.sigmoid(ugc)
            gp_scr[:, lo:lo + MXU_COLS] = gate_p
            gc_scr[:, lo:lo + MXU_COLS] = gate_c
            cw = cw + dep_zero(gate_p[0:1, 0:LANES] + gate_c[0:1, 0:LANES])
        if j == pool_anchor_slab:
            y_pool = pool_branch()
            cw = cw + dep_zero(y_pool[0:1, 0:LANES])
        acc = h_scr[j, pl.ds(HALO - pad, ts), :] * cw[0:1, :]
        for k in range(1, CONV_KERNEL):
            acc = acc + h_scr[j, pl.ds(HALO - pad + k, ts), :] * cw[k:k + 1, :]
        hc_scr[:, j * LANES:(j + 1) * LANES] = acc + conv_b_ref[:, j * LANES:(j + 1) * LANES]

    piece = ts // TAIL_PIECES
    for p in range(TAIL_PIECES):
        rows_p = slice(p * piece, (p + 1) * piece)
        for r in range(piece // LN_ROWS):
            sl = slice(p * piece + r * LN_ROWS, p * piece + (r + 1) * LN_ROWS)
            y = _layer_norm(hc_scr[sl, :], cln_g_ref[...], cln_b_ref[...])
            half_y = (0.5 * y).astype(jnp.bfloat16)
            hs_scr[sl, :] = half_y + half_y * jnp.tanh(half_y)
        y_conv = _dot(hs_scr[rows_p, :], w_cp_ref[...])
        for c in range(n_chunk):
            cols = slice(c * MXU_COLS, (c + 1) * MXU_COLS)
            merged = gp_scr[rows_p, cols] * y_pool[rows_p, cols] + gc_scr[rows_p, cols] * y_conv[:, cols]
            mg_scr[rows_p, cols] = merged.astype(jnp.bfloat16)
        mix = _dot(mg_scr[rows_p, :], w_out_ref[...])
        for r in range(piece // LN_ROWS):
            sl = slice(p * piece + r * LN_ROWS, p * piece + (r + 1) * LN_ROWS)
            res = alpha * x_ref[0, sl, :] + mix[r * LN_ROWS:(r + 1) * LN_ROWS, :]
            o_ref[0, sl, :] = _layer_norm(res, ln_g_ref[...], ln_b_ref[...])


def _ffn_kernel(x_ref, w_in_ref, w_down_ref, ln_g_ref, ln_b_ref, o_ref, xb_scr, hid_scr, *, alpha):
    tm = x_ref.shape[0]
    d_ff = w_down_ref.shape[0]
    xb_scr[...] = x_ref[...].astype(jnp.bfloat16)
    for c in range(d_ff // MXU_COLS):
        lo = c * MXU_COLS
        g = _dot(xb_scr[...], w_in_ref[:, lo:lo + MXU_COLS])
        u = _dot(xb_scr[...], w_in_ref[:, d_ff + lo:d_ff + lo + MXU_COLS])
        hid_scr[:, lo:lo + MXU_COLS] = (g * jax.nn.sigmoid(g) * u).astype(jnp.bfloat16)
    piece = tm // FFN_TAIL_PIECES
    for p in range(FFN_TAIL_PIECES):
        ffn = _dot(hid_scr[p * piece:(p + 1) * piece, :], w_down_ref[...])
        for r in range(piece // LN_ROWS):
            sl = slice(p * piece + r * LN_ROWS, p * piece + (r + 1) * LN_ROWS)
            res = alpha * x_ref[sl, :] + ffn[r * LN_ROWS:(r + 1) * LN_ROWS, :]
            o_ref[sl, :] = _layer_norm(res, ln_g_ref[...], ln_b_ref[...])


def _resident(shape):
    zeros = (0,) * len(shape)
    return pl.BlockSpec(shape, lambda *_: zeros, pipeline_mode=pl.Buffered(1))


def _token_mix(x, w_in, b_in, pool_w, pool_scale, w_pp, conv_w, conv_b, cln_g, cln_b, w_cp,
               w_out, ln_g, ln_b, alpha):
    b, s, d = x.shape
    ts = SEQ_TILE
    assert s % ts == 0 and ts % HALO == 0 and d % MXU_COLS == 0
    assert d // len(POOL_WINDOWS) % LANES == 0
    rows = ts + 2 * HALO
    halo_blocks = ts // HALO
    last_halo_block = s // HALO - 1
    residents = (w_in, b_in, pool_w, pool_scale, w_pp, conv_w, conv_b, cln_g, cln_b, w_cp, w_out, ln_g, ln_b)
    return pl.pallas_call(
        functools.partial(_mix_kernel, seq_len=s, alpha=alpha),
        out_shape=jax.ShapeDtypeStruct((b, s, d), jnp.float32),
        grid=(b, s // ts),
        in_specs=[
            pl.BlockSpec((1, ts, d), lambda bi, i: (bi, i, 0)),
            pl.BlockSpec((1, HALO, d), lambda bi, i: (bi, jnp.maximum(i * halo_blocks - 1, 0), 0)),
            pl.BlockSpec((1, HALO, d), lambda bi, i: (bi, jnp.minimum((i + 1) * halo_blocks, last_halo_block), 0)),
        ] + [_resident(a.shape) for a in residents],
        out_specs=pl.BlockSpec((1, ts, d), lambda bi, i: (bi, i, 0)),
        scratch_shapes=[
            pltpu.VMEM((rows, d), jnp.bfloat16),
            pltpu.VMEM((d // LANES, rows, LANES), jnp.float32),
            pltpu.VMEM((d // LANES, rows, LANES), jnp.float32),
            pltpu.VMEM((ts, d), jnp.bfloat16),
            pltpu.VMEM((ts, d), jnp.bfloat16),
            pltpu.VMEM((ts, d), jnp.float32),
            pltpu.VMEM((ts, d), jnp.bfloat16),
            pltpu.VMEM((ts, d), jnp.bfloat16),
            pltpu.VMEM((ts, d), jnp.float32),
            pltpu.VMEM((ts, d), jnp.float32),
        ],
        compiler_params=pltpu.CompilerParams(
            dimension_semantics=("arbitrary", "arbitrary"),
            vmem_limit_bytes=VMEM_LIMIT_BYTES),
        name="token_mix",
    )(x, x, x, *residents)


def _ffn(x2d, w_in, w_down, ln_g, ln_b, alpha):
    t, d = x2d.shape
    tm = ROW_TILE
    d_ff = w_down.shape[0]
    assert t % tm == 0 and d_ff % MXU_COLS == 0
    residents = (w_in, w_down, ln_g, ln_b)
    return pl.pallas_call(
        functools.partial(_ffn_kernel, alpha=alpha),
        out_shape=jax.ShapeDtypeStruct((t, d), jnp.float32),
        grid=(t // tm,),
        in_specs=[pl.BlockSpec((tm, d), lambda i: (i, 0))] + [_resident(a.shape) for a in residents],
        out_specs=pl.BlockSpec((tm, d), lambda i: (i, 0)),
        scratch_shapes=[
            pltpu.VMEM((tm, d), jnp.bfloat16),
            pltpu.VMEM((tm, d_ff), jnp.bfloat16),
        ],
        compiler_params=pltpu.CompilerParams(
            dimension_semantics=("arbitrary",),
            vmem_limit_bytes=VMEM_LIMIT_BYTES),
        name="swiglu_ffn",
    )(x2d, *residents)


def kernel(x, w_in, b_in, pool_w, pool_scale, w_pool_proj, conv_w, conv_b, conv_ln_g, conv_ln_b, w_conv_proj, w_out, ln1_g, ln1_b, w_ffn_in, w_ffn_down, ln2_g, ln2_b):
    b, s, d = x.shape
    depth = w_in.shape[0]
    alpha = (2.0 * depth) ** 0.25
    bf = lambda w: w.astype(jnp.bfloat16)
    row = lambda v: v.reshape(1, -1)
    for l in range(depth):
        x = _token_mix(
            x, bf(w_in[l]), row(b_in[l]), bf(pool_w[l]), row(pool_scale[l]), bf(w_pool_proj[l]),
            conv_w[l], row(conv_b[l]), row(conv_ln_g[l]), row(conv_ln_b[l]), bf(w_conv_proj[l]),
            bf(w_out[l]), row(ln1_g[l]), row(ln1_b[l]), alpha)
        x = _ffn(x.reshape(b * s, d), bf(w_ffn_in[l]), bf(w_ffn_down[l]), row(ln2_g[l]), row(ln2_b[l]),
                 alpha).reshape(b, s, d)
    return x
```
